```python
import jax, jax.numpy as jnp
from jax import lax
import numpy as np

D_MODEL = 1024
BATCH = 16
SEQ = 4096
DEPTH = 4

N_MIXERS = 2
CHUNK = 64
EPS = 1e-6
D_FF = 2816

GLA_HEADS = 4
GLA_DK = 128
GLA_DV = 256
GLA_KEY = GLA_HEADS * GLA_DK
GLA_VAL = GLA_HEADS * GLA_DV
GLA_GATE_RANK = 16
GLA_GATE_NORM = 16.0
GLA_IN = 2 * GLA_KEY + 2 * GLA_VAL + GLA_GATE_RANK

GDN_HEADS = 8
GDN_DK = 128
GDN_DV = 128
GDN_KEY = GDN_HEADS * GDN_DK
GDN_VAL = GDN_HEADS * GDN_DV
CONV_K = 4
GDN_CONV_CH = 2 * GDN_KEY + GDN_VAL
GDN_IN = GDN_CONV_CH + GDN_VAL + 2 * GDN_HEADS

N_GLA = (DEPTH + N_MIXERS - 1) // N_MIXERS
N_GDN = DEPTH // N_MIXERS

kernel_name = "hybrid_gla_gated_deltanet_macaron"

F32 = jnp.float32


def rmsnorm(x, w):
    xf = x.astype(F32)
    y = xf * lax.rsqrt(jnp.mean(xf * xf, axis=-1, keepdims=True) + EPS)
    return (y * w.astype(F32)).astype(x.dtype)


def swiglu(h, w_in, w_out):
    g, u = jnp.split(h @ w_in, 2, axis=-1)
    return (jax.nn.silu(g) * u) @ w_out


def to_chunks(t, n_heads):
    b, s, hd = t.shape
    return t.reshape(b, s // CHUNK, CHUNK, n_heads, hd // n_heads).transpose(0, 3, 1, 2, 4)


def heads_to_chunks(t):
    b, s, h = t.shape
    return t.reshape(b, s // CHUNK, CHUNK, h).transpose(0, 3, 1, 2)


def from_chunks(t):
    b, h, n, c, d = t.shape
    return t.transpose(0, 2, 3, 1, 4).reshape(b, n * c, h * d)


def head_norm_gate(o, gate, w):
    o = o * lax.rsqrt(jnp.mean(o * o, axis=-1, keepdims=True) + EPS) * w.astype(F32)
    return from_chunks(o).astype(gate.dtype) * jax.nn.silu(gate)


def l2norm(t):
    return t * lax.rsqrt(jnp.sum(t * t, axis=-1, keepdims=True) + EPS)


def causal_conv(x, w):
    c = x.shape[-1]
    return lax.conv_general_dilated(
        x, w[:, None, :].astype(x.dtype), window_strides=(1,), padding=[(CONV_K - 1, 0)],
        dimension_numbers=("NWC", "WIO", "NWC"), feature_group_count=c)


def gla_chunked(q, k, v, g):
    scale = q.shape[-1] ** -0.5
    b = jnp.cumsum(g, axis=3)
    b_last = b[:, :, :, -1:, :]
    q_dec = q * jnp.exp(b) * scale
    k_inv = k * jnp.exp(-b)
    k_end = k * jnp.exp(b_last - b)
    causal = jnp.tril(jnp.ones((CHUNK, CHUNK), dtype=bool))
    a = jnp.where(causal, jnp.einsum("bhnid,bhnjd->bhnij", q_dec, k_inv), 0.0)
    o_intra = jnp.einsum("bhnij,bhnjv->bhniv", a, v)

    def step(state, xs):
        qd, ke, vv, dl = xs
        o = jnp.einsum("bhcd,bhdv->bhcv", qd, state)
        state = dl[..., None] * state + jnp.einsum("bhcd,bhcv->bhdv", ke, vv)
        return state, o

    s0 = jnp.zeros(q.shape[:2] + (q.shape[-1], v.shape[-1]), q.dtype)
    xs = tuple(jnp.moveaxis(t, 2, 0) for t in (q_dec, k_end, v, jnp.exp(b_last[:, :, :, 0])))
    _, o_inter = lax.scan(step, s0, xs)
    return o_intra + jnp.moveaxis(o_inter, 0, 2)


def gdn_chunked(q, k, v, beta, g):
    dv = v.shape[-1]
    q = q * (q.shape[-1] ** -0.5)
    d = jnp.cumsum(g, axis=-1)
    d_last = d[..., -1]
    causal = jnp.tril(jnp.ones((CHUNK, CHUNK), dtype=bool))
    strict = jnp.tril(jnp.ones((CHUNK, CHUNK), dtype=bool), k=-1)
    decay = jnp.where(causal, jnp.exp(jnp.minimum(d[..., :, None] - d[..., None, :], 0.0)), 0.0)
    kb = k * beta[..., None]
    m = jnp.where(strict, jnp.einsum("bhnid,bhnjd->bhnij", kb, k) * decay, 0.0)
    a = m + jnp.eye(CHUNK, dtype=m.dtype)
    rhs = jnp.concatenate([v * beta[..., None], kb * jnp.exp(d)[..., None]], axis=-1)
    sol = lax.linalg.triangular_solve(a, rhs, left_side=True, lower=True, unit_diagonal=True)
    u, w = sol[..., :dv], sol[..., dv:]
    qk = jnp.einsum("bhnid,bhnjd->bhnij", q, k) * decay
    q_dec = q * jnp.exp(d)[..., None]
    k_end = k * jnp.exp(d_last[..., None] - d)[..., None]

    def step(state, xs):
        qk_c, qd, ke, u_c, w_c, dl = xs
        v_new = u_c - jnp.einsum("bhcd,bhdv->bhcv", w_c, state)
        o = jnp.einsum("bhcd,bhdv->bhcv", qd, state) + jnp.einsum("bhij,bhjv->bhiv", qk_c, v_new)
        state = dl[..., None, None] * state + jnp.einsum("bhcd,bhcv->bhdv", ke, v_new)
        return state, o

    s0 = jnp.zeros(q.shape[:2] + (q.shape[-1], dv), q.dtype)
    xs = tuple(jnp.moveaxis(t, 2, 0) for t in (qk, q_dec, k_end, u, w, jnp.exp(d_last)))
    _, o = lax.scan(step, s0, xs)
    return jnp.moveaxis(o, 0, 2)


def gla_mixer(h, w_in, w_gk, b_gk, w_norm, w_out):
    p = h @ w_in
    q, k, v, r, z = jnp.split(
        p, [GLA_KEY, 2 * GLA_KEY, 2 * GLA_KEY + GLA_VAL, 2 * GLA_KEY + 2 * GLA_VAL], axis=-1)
    gk = jax.nn.log_sigmoid((z @ w_gk + b_gk).astype(F32)) / GLA_GATE_NORM
    o = gla_chunked(to_chunks(q.astype(F32), GLA_HEADS), to_chunks(k.astype(F32), GLA_HEADS),
                    to_chunks(v.astype(F32), GLA_HEADS), to_chunks(gk, GLA_HEADS))
    return head_norm_gate(o, r, w_norm) @ w_out


def gdn_mixer(h, w_in, conv_w, a_log, dt_bias, w_norm, w_out):
    p = h @ w_in
    qkv, r, ab = jnp.split(p, [GDN_CONV_CH, GDN_CONV_CH + GDN_VAL], axis=-1)
    qkv = jax.nn.silu(causal_conv(qkv, conv_w))
    q, k, v = jnp.split(qkv.astype(F32), [GDN_KEY, 2 * GDN_KEY], axis=-1)
    a, bl = jnp.split(ab.astype(F32), 2, axis=-1)
    beta = jax.nn.sigmoid(bl)
    g = -jnp.exp(a_log.astype(F32)) * jax.nn.softplus(a + dt_bias.astype(F32))
    o = gdn_chunked(l2norm(to_chunks(q, GDN_HEADS)), l2norm(to_chunks(k, GDN_HEADS)),
                    to_chunks(v, GDN_HEADS), heads_to_chunks(beta), heads_to_chunks(g))
    return head_norm_gate(o, r, w_norm) @ w_out


def setup_inputs(seed: int = 0) -> dict:
    key = jax.random.key(seed)
    ks = jax.random.split(key, 20)
    nrm = lambda k, shape, fan_in: jax.random.normal(k, shape, F32) * (fan_in ** -0.5)
    x = jax.random.normal(ks[0], (BATCH, SEQ, D_MODEL), F32)
    norm_w = 1.0 + 0.02 * jax.random.normal(ks[1], (DEPTH, 3, D_MODEL), F32)
    ffn_w_in = nrm(ks[2], (DEPTH, 2, D_MODEL, 2 * D_FF), D_MODEL)
    ffn_w_out = nrm(ks[3], (DEPTH, 2, D_FF, D_MODEL), D_FF)
    gla_w_in = nrm(ks[4], (N_GLA, D_MODEL, GLA_IN), D_MODEL)
    gla_w_gk = nrm(ks[5], (N_GLA, GLA_GATE_RANK, GLA_KEY), GLA_GATE_RANK)
    gla_b_gk = 0.1 * jax.random.normal(ks[6], (N_GLA, GLA_KEY), F32)
    gla_norm_w = 1.0 + 0.02 * jax.random.normal(ks[7], (N_GLA, GLA_DV), F32)
    gla_w_out = nrm(ks[8], (N_GLA, GLA_VAL, D_MODEL), GLA_VAL)
    gdn_w_in = nrm(ks[9], (N_GDN, D_MODEL, GDN_IN), D_MODEL)
    gdn_conv_w = nrm(ks[10], (N_GDN, CONV_K, GDN_CONV_CH), CONV_K)
    gdn_a_log = jnp.log(jax.random.uniform(ks[11], (N_GDN, GDN_HEADS), F32, minval=1.0, maxval=16.0))
    dt = jnp.exp(jax.random.uniform(ks[12], (N_GDN, GDN_HEADS), F32,
                                    minval=float(np.log(1e-3)), maxval=float(np.log(1e-1))))
    gdn_dt_bias = jnp.log(jnp.expm1(dt))
    gdn_norm_w = 1.0 + 0.02 * jax.random.normal(ks[13], (N_GDN, GDN_DV), F32)
    gdn_w_out = nrm(ks[14], (N_GDN, GDN_VAL, D_MODEL), GDN_VAL)
    final_norm_w = 1.0 + 0.02 * jax.random.normal(ks[15], (D_MODEL,), F32)
    return {"x": x, "norm_w": norm_w, "ffn_w_in": ffn_w_in, "ffn_w_out": ffn_w_out,
            "gla_w_in": gla_w_in, "gla_w_gk": gla_w_gk, "gla_b_gk": gla_b_gk,
            "gla_norm_w": gla_norm_w, "gla_w_out": gla_w_out,
            "gdn_w_in": gdn_w_in, "gdn_conv_w": gdn_conv_w, "gdn_a_log": gdn_a_log,
            "gdn_dt_bias": gdn_dt_bias, "gdn_norm_w": gdn_norm_w, "gdn_w_out": gdn_w_out,
            "final_norm_w": final_norm_w}


def reference(x, norm_w, ffn_w_in, ffn_w_out, gla_w_in, gla_w_gk, gla_b_gk, gla_norm_w, gla_w_out,
              gdn_w_in, gdn_conv_w, gdn_a_log, gdn_dt_bias, gdn_norm_w, gdn_w_out, final_norm_w):
    for i in range(DEPTH):
        x = x + 0.5 * swiglu(rmsnorm(x, norm_w[i, 0]), ffn_w_in[i, 0], ffn_w_out[i, 0])
        h = rmsnorm(x, norm_w[i, 1])
        j = i // N_MIXERS
        if i % N_MIXERS == 0:
            x = x + gla_mixer(h, gla_w_in[j], gla_w_gk[j], gla_b_gk[j], gla_norm_w[j], gla_w_out[j])
        else:
            x = x + gdn_mixer(h, gdn_w_in[j], gdn_conv_w[j], gdn_a_log[j], gdn_dt_bias[j],
                              gdn_norm_w[j], gdn_w_out[j])
        x = x + 0.5 * swiglu(rmsnorm(x, norm_w[i, 2]), ffn_w_in[i, 1], ffn_w_out[i, 1])
    return rmsnorm(x, final_norm_w)
```

```python
import functools

import jax
import jax.numpy as jnp
from jax import lax
from jax.experimental import pallas as pl
from jax.experimental.pallas import tpu as pltpu

F32 = jnp.float32
BF16 = jnp.bfloat16

EPS = 1e-6
CHUNK = 64
LANES = 128
SUBLANES = 8
MXU_N = 256

GLA_HEADS, GLA_DK, GLA_DV = 4, 128, 256
GLA_GATE_NORM = 16.0
GDN_HEADS, GDN_DK, GDN_DV = 8, 128, 128
CONV_K = 4

VMEM_LIMIT_BYTES = 60000 * 1024


def _rmsnorm(x, w):
    return x * lax.rsqrt(jnp.mean(x * x, axis=-1, keepdims=True) + EPS) * w


def _dot(a, b):
    return jnp.dot(a.astype(BF16), b.astype(BF16), preferred_element_type=F32)


def _dot_nt(a, b):
    return lax.dot_general(a.astype(BF16), b.astype(BF16), (((1,), (1,)), ((), ())),
                           preferred_element_type=F32)


def _dot_tn(a, b):
    return jnp.dot(a.astype(F32).T.astype(BF16), b.astype(BF16), preferred_element_type=F32)


def _softplus(x):
    return jnp.maximum(x, 0.0) + jnp.log1p(jnp.exp(-jnp.abs(x)))


def _cumsum_rows(x):
    rows = x.shape[0]
    row = lax.broadcasted_iota(jnp.int32, x.shape, 0)
    shift = 1
    while shift < rows:
        x = x + jnp.where(row >= shift, pltpu.roll(x, shift, axis=0), 0.0)
        shift *= 2
    return x


def _ffn_kernel(x_ref, nw_ref, wg_ref, wu_ref, wo_ref, fw_ref, o_ref, h_ref, acc_ref, *,
                n_chunks, final_norm):
    h_ref[...] = _rmsnorm(x_ref[...], nw_ref[...]).astype(BF16)
    acc_ref[...] = jnp.zeros_like(acc_ref)

    def body(c, carry):
        h = h_ref[...]
        g = jnp.dot(h, wg_ref[c], preferred_element_type=F32)
        u = jnp.dot(h, wu_ref[c], preferred_element_type=F32)
        a = (g * jax.nn.sigmoid(g) * u).astype(BF16)
        acc_ref[...] += jnp.dot(a, wo_ref[c], preferred_element_type=F32)
        return carry

    lax.fori_loop(0, n_chunks, body, 0)
    y = x_ref[...] + 0.5 * acc_ref[...]
    if final_norm:
        y = _rmsnorm(y, fw_ref[...])
    o_ref[...] = y


def _resident(shape):
    nd = len(shape)
    return pl.BlockSpec(shape, lambda *_: (0,) * nd, pipeline_mode=pl.Buffered(1))


def _ffn(x2d, nw, w_in, w_out, final_w, *, tm, final_norm):
    m, d = x2d.shape
    f = w_out.shape[0]
    tf = MXU_N
    n_chunks = f // tf
    assert f == n_chunks * tf and m % tm == 0
    wg = w_in[:, :f].astype(BF16).reshape(d, n_chunks, tf).transpose(1, 0, 2)
    wu = w_in[:, f:].astype(BF16).reshape(d, n_chunks, tf).transpose(1, 0, 2)
    wo = w_out.astype(BF16).reshape(n_chunks, tf, d)
    kern = functools.partial(_ffn_kernel, n_chunks=n_chunks, final_norm=final_norm)
    return pl.pallas_call(
        kern,
        grid=(m // tm,),
        in_specs=[
            pl.BlockSpec((tm, d), lambda i: (i, 0)),
            _resident((1, d)),
            _resident((n_chunks, d, tf)),
            _resident((n_chunks, d, tf)),
            _resident((n_chunks, tf, d)),
            _resident((1, d)),
        ],
        out_specs=pl.BlockSpec((tm, d), lambda i: (i, 0)),
        out_shape=jax.ShapeDtypeStruct((m, d), F32),
        scratch_shapes=[pltpu.VMEM((tm, d), BF16), pltpu.VMEM((tm, d), F32)],
        compiler_params=pltpu.CompilerParams(
            dimension_semantics=("arbitrary",), vmem_limit_bytes=VMEM_LIMIT_BYTES),
        name="ffn",
    )(x2d, nw.reshape(1, d), wg, wu, wo, final_w.reshape(1, d))


def _gla_kernel(x_ref, nw_ref, wq_ref, wk_ref, wv_ref, wr_ref, wz_ref, wgk_ref, bgk_ref,
                gw_ref, wo_ref, o_ref,
                h_ref, q_ref, k_ref, g_ref, v_ref, att_ref, state_ref, *, n_chunks):
    @pl.when(pl.program_id(1) == 0)
    def _():
        state_ref[...] = jnp.zeros_like(state_ref)

    h = _rmsnorm(x_ref[...], nw_ref[...]).astype(BF16)
    h_ref[...] = h
    q_ref[...] = jnp.dot(h, wq_ref[...], preferred_element_type=F32)
    k_ref[...] = jnp.dot(h, wk_ref[...], preferred_element_type=F32)
    v_ref[...] = jnp.dot(h, wv_ref[...], preferred_element_type=F32)
    z = jnp.dot(h, wz_ref[...], preferred_element_type=F32)
    g_ref[...] = -_softplus(-(_dot(z, wgk_ref[...]) + bgk_ref[...])) / GLA_GATE_NORM

    scale = GLA_DK ** -0.5
    causal = (lax.broadcasted_iota(jnp.int32, (CHUNK, CHUNK), 0)
              >= lax.broadcasted_iota(jnp.int32, (CHUNK, CHUNK), 1))

    def chunk_body(c, carry):
        rows = pl.ds(pl.multiple_of(c * CHUNK, CHUNK), CHUNK)
        b = _cumsum_rows(g_ref[rows, :])
        b_last = b[CHUNK - 1:CHUNK, :]
        q = q_ref[rows, :]
        k = k_ref[rows, :]
        q_dec = q * jnp.exp(b) * scale
        k_inv = k * jnp.exp(-b)
        k_end = k * jnp.exp(b_last - b)
        decay_t = jnp.exp(b_last)
        for hd in range(GLA_HEADS):
            kl = slice(hd * GLA_DK, (hd + 1) * GLA_DK)
            vl = slice(hd * GLA_DV, (hd + 1) * GLA_DV)
            a = jnp.where(causal, _dot_nt(q_dec[:, kl], k_inv[:, kl]), 0.0)
            vh = v_ref[rows, vl]
            st = state_ref[hd]
            att_ref[rows, vl] = _dot(a, vh) + _dot_nt(q_dec[:, kl], st)
            state_ref[hd] = st * decay_t[:, kl] + _dot_tn(vh, k_end[:, kl])
        return carry

    lax.fori_loop(0, n_chunks, chunk_body, 0)

    o = att_ref[...]
    parts = []
    for hd in range(GLA_HEADS):
        oh = o[:, hd * GLA_DV:(hd + 1) * GLA_DV]
        parts.append(oh * lax.rsqrt(jnp.mean(oh * oh, axis=-1, keepdims=True) + EPS))
    on = jnp.concatenate(parts, axis=-1) * gw_ref[...]
    r = jnp.dot(h_ref[...], wr_ref[...], preferred_element_type=F32)
    y = on * (r * jax.nn.sigmoid(r))
    o_ref[...] = x_ref[...] + jnp.dot(y.astype(BF16), wo_ref[...], preferred_element_type=F32)


def _gla_layer(x, nw, w_in, w_gk, b_gk, gnorm_w, w_out, *, ts):
    bsz, s, d = x.shape
    key = GLA_HEADS * GLA_DK
    val = GLA_HEADS * GLA_DV
    rank = w_gk.shape[0]
    assert s % ts == 0 and ts % CHUNK == 0
    w = w_in.astype(BF16)
    wq, wk = w[:, :key], w[:, key:2 * key]
    wv, wr = w[:, 2 * key:2 * key + val], w[:, 2 * key + val:2 * key + 2 * val]
    wz = jnp.pad(w[:, 2 * key + 2 * val:], ((0, 0), (0, LANES - rank)))
    wgk = jnp.pad(w_gk.astype(BF16), ((0, LANES - rank), (0, 0)))
    kern = functools.partial(_gla_kernel, n_chunks=ts // CHUNK)
    xspec = pl.BlockSpec((None, ts, d), lambda b, j: (b, j, 0))
    return pl.pallas_call(
        kern,
        grid=(bsz, s // ts),
        in_specs=[
            xspec,
            _resident((1, d)),
            _resident((d, key)), _resident((d, key)), _resident((d, val)), _resident((d, val)),
            _resident((d, LANES)), _resident((LANES, key)), _resident((1, key)),
            _resident((1, val)), _resident((val, d)),
        ],
        out_specs=xspec,
        out_shape=jax.ShapeDtypeStruct(x.shape, F32),
        scratch_shapes=[
            pltpu.VMEM((ts, d), BF16),
            pltpu.VMEM((ts, key), F32),
            pltpu.VMEM((ts, key), F32),
            pltpu.VMEM((ts, key), F32),
            pltpu.VMEM((ts, val), F32),
            pltpu.VMEM((ts, val), F32),
            pltpu.VMEM((GLA_HEADS, GLA_DV, GLA_DK), F32),
        ],
        compiler_params=pltpu.CompilerParams(
            dimension_semantics=("arbitrary", "arbitrary"), vmem_limit_bytes=VMEM_LIMIT_BYTES),
        name="gla_layer",
    )(x, nw.reshape(1, d), wq, wk, wv, wr, wz, wgk, b_gk.reshape(1, key),
      jnp.tile(gnorm_w, GLA_HEADS).reshape(1, val), w_out.astype(BF16))


def _gdn_kernel(x_ref, nw_ref, wqkv_ref, wr_ref, wa_ref, wb_ref, cw_ref, alog_ref, dtb_ref,
                gw_ref, wo_ref, o_ref,
                h_ref, q_ref, k_ref, v_ref, g_ref, beta_ref, att_ref, tail_ref, state_ref, *,
                n_chunks):
    ts = x_ref.shape[0]
    key = GDN_HEADS * GDN_DK

    @pl.when(pl.program_id(1) == 0)
    def _():
        state_ref[...] = jnp.zeros_like(state_ref)
        tail_ref[...] = jnp.zeros_like(tail_ref)

    h = _rmsnorm(x_ref[...], nw_ref[...]).astype(BF16)
    h_ref[...] = h

    pre = jnp.dot(h, wqkv_ref[...], preferred_element_type=F32)
    ext = jnp.concatenate([tail_ref[...], pre], axis=0)
    tail_ref[...] = pre[ts - SUBLANES:, :]
    cw = cw_ref[...]
    conv = pre * cw[CONV_K - 1:CONV_K, :]
    for back in range(1, CONV_K):
        shifted = pltpu.roll(ext, back, axis=0)[SUBLANES:, :]
        conv = conv + shifted * cw[CONV_K - 1 - back:CONV_K - back, :]
    qkv = conv * jax.nn.sigmoid(conv)

    scale = GDN_DK ** -0.5
    for hd in range(GDN_HEADS):
        ql = slice(hd * GDN_DK, (hd + 1) * GDN_DK)
        kl = slice(key + hd * GDN_DK, key + (hd + 1) * GDN_DK)
        qh = qkv[:, ql]
        kh = qkv[:, kl]
        q_ref[:, ql] = qh * lax.rsqrt(jnp.sum(qh * qh, axis=-1, keepdims=True) + EPS) * scale
        k_ref[:, ql] = kh * lax.rsqrt(jnp.sum(kh * kh, axis=-1, keepdims=True) + EPS)
    v_ref[...] = qkv[:, 2 * key:]

    a = jnp.dot(h, wa_ref[...], preferred_element_type=F32)
    bl = jnp.dot(h, wb_ref[...], preferred_element_type=F32)
    beta_ref[...] = jax.nn.sigmoid(bl)
    g_ref[...] = -jnp.exp(alog_ref[...]) * _softplus(a + dtb_ref[...])

    ri = lax.broadcasted_iota(jnp.int32, (CHUNK, CHUNK), 0)
    ci = lax.broadcasted_iota(jnp.int32, (CHUNK, CHUNK), 1)
    causal = ri >= ci
    strict = ri > ci

    def chunk_body(c, carry):
        rows = pl.ds(pl.multiple_of(c * CHUNK, CHUNK), CHUNK)
        dcum = _cumsum_rows(g_ref[rows, :])
        dcum_t = dcum.T
        beta = beta_ref[rows, :]
        e_d = jnp.exp(dcum)
        d_last = dcum[CHUNK - 1:CHUNK, :]
        e_end = jnp.exp(d_last - dcum)
        e_last = jnp.exp(d_last)
        for hd in range(GDN_HEADS):
            hl = slice(hd * GDN_DK, (hd + 1) * GDN_DK)
            q = q_ref[rows, hl]
            k = k_ref[rows, hl]
            v = v_ref[rows, hl]
            beta_c = beta[:, hd:hd + 1]
            decay = jnp.where(
                causal, jnp.exp(jnp.minimum(dcum[:, hd:hd + 1] - dcum_t[hd:hd + 1, :], 0.0)), 0.0)
            kb = k * beta_c
            neg_m = jnp.where(strict, -(_dot_nt(kb, k) * decay), 0.0)
            qk = _dot_nt(q, k) * decay
            sol = jnp.concatenate([v * beta_c, kb * e_d[:, hd:hd + 1]], axis=-1)
            power = neg_m
            span = 1
            while True:
                sol = sol + _dot(power, sol)
                span *= 2
                if span >= CHUNK:
                    break
                power = _dot(power, power)
            u = sol[:, :GDN_DV]
            w = sol[:, GDN_DV:]
            st = state_ref[hd]
            v_new = u - _dot(w, st)
            att_ref[rows, hl] = _dot(q * e_d[:, hd:hd + 1], st) + _dot(qk, v_new)
            state_ref[hd] = st * e_last[:, hd:hd + 1] + _dot_tn(k * e_end[:, hd:hd + 1], v_new)
        return carry

    lax.fori_loop(0, n_chunks, chunk_body, 0)

    o = att_ref[...]
    parts = []
    for hd in range(GDN_HEADS):
        oh = o[:, hd * GDN_DV:(hd + 1) * GDN_DV]
        parts.append(oh * lax.rsqrt(jnp.mean(oh * oh, axis=-1, keepdims=True) + EPS))
    on = jnp.concatenate(parts, axis=-1) * gw_ref[...]
    r = jnp.dot(h_ref[...], wr_ref[...], preferred_element_type=F32)
    y = on * (r * jax.nn.sigmoid(r))
    o_ref[...] = x_ref[...] + jnp.dot(y.astype(BF16), wo_ref[...], preferred_element_type=F32)


def _gdn_layer(x, nw, w_in, conv_w, a_log, dt_bias, gnorm_w, w_out, *, ts):
    bsz, s, d = x.shape
    key = GDN_HEADS * GDN_DK
    val = GDN_HEADS * GDN_DV
    conv_ch = 2 * key + val
    assert s % ts == 0 and ts % CHUNK == 0
    w = w_in.astype(BF16)
    wqkv, wr = w[:, :conv_ch], w[:, conv_ch:conv_ch + val]
    pad = ((0, 0), (0, LANES - GDN_HEADS))
    wa = jnp.pad(w[:, conv_ch + val:conv_ch + val + GDN_HEADS], pad)
    wb = jnp.pad(w[:, conv_ch + val + GDN_HEADS:], pad)
    alog = jnp.pad(a_log.reshape(1, GDN_HEADS), pad)
    dtb = jnp.pad(dt_bias.reshape(1, GDN_HEADS), pad)
    kern = functools.partial(_gdn_kernel, n_chunks=ts // CHUNK)
    xspec = pl.BlockSpec((None, ts, d), lambda b, j: (b, j, 0))
    return pl.pallas_call(
        kern,
        grid=(bsz, s // ts),
        in_specs=[
            xspec,
            _resident((1, d)),
            _resident((d, conv_ch)), _resident((d, val)), _resident((d, LANES)), _resident((d, LANES)),
            _resident((CONV_K, conv_ch)), _resident((1, LANES)), _resident((1, LANES)),
            _resident((1, val)), _resident((val, d)),
        ],
        out_specs=xspec,
        out_shape=jax.ShapeDtypeStruct(x.shape, F32),
        scratch_shapes=[
            pltpu.VMEM((ts, d), BF16),
            pltpu.VMEM((ts, key), F32),
            pltpu.VMEM((ts, key), F32),
            pltpu.VMEM((ts, val), F32),
            pltpu.VMEM((ts, LANES), F32),
            pltpu.VMEM((ts, LANES), F32),
            pltpu.VMEM((ts, val), F32),
            pltpu.VMEM((SUBLANES, conv_ch), F32),
            pltpu.VMEM((GDN_HEADS, GDN_DK, GDN_DV), F32),
        ],
        compiler_params=pltpu.CompilerParams(
            dimension_semantics=("arbitrary", "arbitrary"), vmem_limit_bytes=VMEM_LIMIT_BYTES),
        name="gdn_layer",
    )(x, nw.reshape(1, d), wqkv, wr, wa, wb, conv_w, alog, dtb,
      jnp.tile(gnorm_w, GDN_HEADS).reshape(1, val), w_out.astype(BF16))


def _trunk(x, norm_w, ffn_w_in, ffn_w_out, gla_w_in, gla_w_gk, gla_b_gk, gla_norm_w, gla_w_out,
           gdn_w_in, gdn_conv_w, gdn_a_log, gdn_dt_bias, gdn_norm_w, gdn_w_out, final_norm_w,
           *, tm, ts):
    bsz, s, d = x.shape
    depth = norm_w.shape[0]
    for i in range(depth):
        x = _ffn(x.reshape(bsz * s, d), norm_w[i, 0], ffn_w_in[i, 0], ffn_w_out[i, 0],
                 final_norm_w, tm=tm, final_norm=False).reshape(bsz, s, d)
        j = i // 2
        if i % 2 == 0:
            x = _gla_layer(x, norm_w[i, 1], gla_w_in[j], gla_w_gk[j], gla_b_gk[j], gla_norm_w[j],
                           gla_w_out[j], ts=ts)
        else:
            x = _gdn_layer(x, norm_w[i, 1], gdn_w_in[j], gdn_conv_w[j], gdn_a_log[j],
                           gdn_dt_bias[j], gdn_norm_w[j], gdn_w_out[j], ts=ts)
        x = _ffn(x.reshape(bsz * s, d), norm_w[i, 2], ffn_w_in[i, 1], ffn_w_out[i, 1],
                 final_norm_w, tm=tm, final_norm=(i == depth - 1)).reshape(bsz, s, d)
    return x


def kernel(x, norm_w, ffn_w_in, ffn_w_out, gla_w_in, gla_w_gk, gla_b_gk, gla_norm_w, gla_w_out,
           gdn_w_in, gdn_conv_w, gdn_a_log, gdn_dt_bias, gdn_norm_w, gdn_w_out, final_norm_w):
    return _trunk(x, norm_w, ffn_w_in, ffn_w_out, gla_w_in, gla_w_gk, gla_b_gk, gla_norm_w,
                  gla_w_out, gdn_w_in, gdn_conv_w, gdn_a_log, gdn_dt_bias, gdn_norm_w, gdn_w_out,
                  final_norm_w, tm=1024, ts=512)
```

```python
import functools

import jax
import jax.numpy as jnp
from jax import lax
from jax.experimental import pallas as pl
from jax.experimental.pallas import tpu as pltpu

F32 = jnp.float32
BF16 = jnp.bfloat16

EPS = 1e-6
CHUNK = 64
LANES = 128
SUBLANES = 8
MXU_N = 256

GLA_HEADS, GLA_DK, GLA_DV = 4, 128, 256
GLA_GATE_NORM = 16.0
GDN_HEADS, GDN_DK, GDN_DV = 8, 128, 128
CONV_K = 4

VMEM_LIMIT_BYTES = 60000 * 1024


def _rmsnorm(x, w):
    return x * lax.rsqrt(jnp.mean(x * x, axis=-1, keepdims=True) + EPS) * w


def _dot(a, b):
    return jnp.dot(a.astype(BF16), b.astype(BF16), preferred_element_type=F32)


def _dot_nt(a, b):
    return lax.dot_general(a.astype(BF16), b.astype(BF16), (((1,), (1,)), ((), ())),
                           preferred_element_type=F32)


def _dot_tn(a, b):
    return jnp.dot(a.astype(F32).T.astype(BF16), b.astype(BF16), preferred_element_type=F32)


def _softplus(x):
    return jnp.maximum(x, 0.0) + jnp.log1p(jnp.exp(-jnp.abs(x)))


def _cumsum_rows(x):
    rows = x.shape[0]
    row = lax.broadcasted_iota(jnp.int32, x.shape, 0)
    shift = 1
    while shift < rows:
        x = x + jnp.where(row >= shift, pltpu.roll(x, shift, axis=0), 0.0)
        shift *= 2
    return x


def _ffn_kernel(x_ref, nw_ref, wg_ref, wu_ref, wo_ref, fw_ref, o_ref, h_ref, acc_ref, *,
                n_chunks, final_norm):
    h_ref[...] = _rmsnorm(x_ref[...], nw_ref[...]).astype(BF16)
    acc_ref[...] = jnp.zeros_like(acc_ref)

    def body(c, carry):
        h = h_ref[...]
        g = jnp.dot(h, wg_ref[c], preferred_element_type=F32)
        u = jnp.dot(h, wu_ref[c], preferred_element_type=F32)
        a = (g * jax.nn.sigmoid(g) * u).astype(BF16)
        acc_ref[...] += jnp.dot(a, wo_ref[c], preferred_element_type=F32)
        return carry

    lax.fori_loop(0, n_chunks, body, 0, unroll=True)
    y = x_ref[...] + 0.5 * acc_ref[...]
    if final_norm:
        y = _rmsnorm(y, fw_ref[...])
    o_ref[...] = y


def _resident(shape):
    nd = len(shape)
    return pl.BlockSpec(shape, lambda *_: (0,) * nd, pipeline_mode=pl.Buffered(1))


def _ffn(x2d, nw, w_in, w_out, final_w, *, tm, final_norm):
    m, d = x2d.shape
    f = w_out.shape[0]
    tf = MXU_N
    n_chunks = f // tf
    assert f == n_chunks * tf and m % tm == 0
    wg = w_in[:, :f].astype(BF16).reshape(d, n_chunks, tf).transpose(1, 0, 2)
    wu = w_in[:, f:].astype(BF16).reshape(d, n_chunks, tf).transpose(1, 0, 2)
    wo = w_out.astype(BF16).reshape(n_chunks, tf, d)
    kern = functools.partial(_ffn_kernel, n_chunks=n_chunks, final_norm=final_norm)
    return pl.pallas_call(
        kern,
        grid=(m // tm,),
        in_specs=[
            pl.BlockSpec((tm, d), lambda i: (i, 0)),
            _resident((1, d)),
            _resident((n_chunks, d, tf)),
            _resident((n_chunks, d, tf)),
            _resident((n_chunks, tf, d)),
            _resident((1, d)),
        ],
        out_specs=pl.BlockSpec((tm, d), lambda i: (i, 0)),
        out_shape=jax.ShapeDtypeStruct((m, d), F32),
        scratch_shapes=[pltpu.VMEM((tm, d), BF16), pltpu.VMEM((tm, d), F32)],
        compiler_params=pltpu.CompilerParams(
            dimension_semantics=("arbitrary",), vmem_limit_bytes=VMEM_LIMIT_BYTES),
        name="ffn",
    )(x2d, nw.reshape(1, d), wg, wu, wo, final_w.reshape(1, d))


def _gla_kernel(x_ref, nw_ref, wq_ref, wk_ref, wv_ref, wr_ref, wz_ref, wgk_ref, bgk_ref,
                gw_ref, wo_ref, o_ref,
                h_ref, q_ref, k_ref, g_ref, v_ref, att_ref, qa_ref, kendt_ref, decay_ref, state_ref,
                *, n_chunks):
    @pl.when(pl.program_id(1) == 0)
    def _():
        state_ref[...] = jnp.zeros_like(state_ref)

    h = _rmsnorm(x_ref[...], nw_ref[...]).astype(BF16)
    h_ref[...] = h
    q_ref[...] = jnp.dot(h, wq_ref[...], preferred_element_type=F32)
    k_ref[...] = jnp.dot(h, wk_ref[...], preferred_element_type=F32)
    v_ref[...] = jnp.dot(h, wv_ref[...], preferred_element_type=F32).astype(BF16)
    z = jnp.dot(h, wz_ref[...], preferred_element_type=F32)
    g_ref[...] = -_softplus(-(_dot(z, wgk_ref[...]) + bgk_ref[...])) / GLA_GATE_NORM

    scale = GLA_DK ** -0.5
    causal = (lax.broadcasted_iota(jnp.int32, (CHUNK, LANES), 0)
              >= lax.broadcasted_iota(jnp.int32, (CHUNK, LANES), 1))
    zeros_c = jnp.zeros((CHUNK, LANES), F32)
    zeros_v = jnp.zeros((CHUNK, GLA_DV), BF16)
    heads = range(GLA_HEADS)
    kls = [slice(hd * GLA_DK, (hd + 1) * GLA_DK) for hd in heads]
    vls = [slice(hd * GLA_DV, (hd + 1) * GLA_DV) for hd in heads]

    def prep_body(c, carry):
        rows = pl.ds(pl.multiple_of(c * CHUNK, CHUNK), CHUNK)
        rows2 = pl.ds(pl.multiple_of(c * 2 * CHUNK, 2 * CHUNK), 2 * CHUNK)
        b = _cumsum_rows(g_ref[rows, :])
        b_last = b[CHUNK - 1:CHUNK, :]
        q = q_ref[rows, :]
        k = k_ref[rows, :]
        q_dec = q * jnp.exp(b) * scale
        k_inv = k * jnp.exp(-b)
        k_end = k * jnp.exp(b_last - b)
        decay = jnp.exp(b_last)
        scores = [_dot_nt(q_dec[:, kl], jnp.concatenate([k_inv[:, kl], zeros_c], axis=0)) for kl in kls]
        for hd in heads:
            a = jnp.where(causal, scores[hd], 0.0)
            qa_ref[rows, vls[hd]] = jnp.concatenate([q_dec[:, kls[hd]], a], axis=-1).astype(BF16)
            kendt_ref[rows2, hd * LANES:hd * LANES + CHUNK] = k_end[:, kls[hd]].T.astype(BF16)
            dslot = pl.ds(pl.multiple_of((c * GLA_HEADS + hd) * GLA_DK, GLA_DK), GLA_DK)
            decay_ref[dslot, :] = jnp.broadcast_to(decay[:, kls[hd]], (GLA_DK, GLA_DK)).T
        return carry

    lax.fori_loop(0, n_chunks, prep_body, 0)

    def scan_body(c, carry):
        rows = pl.ds(pl.multiple_of(c * CHUNK, CHUNK), CHUNK)
        rows2 = pl.ds(pl.multiple_of(c * 2 * CHUNK, 2 * CHUNK), 2 * CHUNK)
        sts = [state_ref[hd] for hd in heads]
        vs = [v_ref[rows, vl] for vl in vls]
        for hd in heads:
            rhs = jnp.concatenate([sts[hd].astype(BF16), vs[hd], zeros_v], axis=0)
            att_ref[rows, vls[hd]] = jnp.dot(qa_ref[rows, vls[hd]], rhs, preferred_element_type=F32)
        for hd in heads:
            dslot = pl.ds(pl.multiple_of((c * GLA_HEADS + hd) * GLA_DK, GLA_DK), GLA_DK)
            dec = decay_ref[dslot, :]
            kend_t = kendt_ref[rows2, hd * LANES:hd * LANES + CHUNK]
            state_ref[hd] = (sts[hd] * jnp.concatenate([dec, dec], axis=-1)
                             + jnp.dot(kend_t, vs[hd], preferred_element_type=F32))
        return carry

    lax.fori_loop(0, n_chunks, scan_body, 0)

    o = att_ref[...]
    parts = []
    for hd in range(GLA_HEADS):
        oh = o[:, hd * GLA_DV:(hd + 1) * GLA_DV]
        parts.append(oh * lax.rsqrt(jnp.mean(oh * oh, axis=-1, keepdims=True) + EPS))
    on = jnp.concatenate(parts, axis=-1) * gw_ref[...]
    r = jnp.dot(h_ref[...], wr_ref[...], preferred_element_type=F32)
    y = on * (r * jax.nn.sigmoid(r))
    o_ref[...] = x_ref[...] + jnp.dot(y.astype(BF16), wo_ref[...], preferred_element_type=F32)


def _gla_layer(x, nw, w_in, w_gk, b_gk, gnorm_w, w_out, *, ts):
    bsz, s, d = x.shape
    key = GLA_HEADS * GLA_DK
    val = GLA_HEADS * GLA_DV
    rank = w_gk.shape[0]
    assert s % ts == 0 and ts % CHUNK == 0
    w = w_in.astype(BF16)
    wq, wk = w[:, :key], w[:, key:2 * key]
    wv, wr = w[:, 2 * key:2 * key + val], w[:, 2 * key + val:2 * key + 2 * val]
    wz = jnp.pad(w[:, 2 * key + 2 * val:], ((0, 0), (0, LANES - rank)))
    wgk = jnp.pad(w_gk.astype(BF16), ((0, LANES - rank), (0, 0)))
    kern = functools.partial(_gla_kernel, n_chunks=ts // CHUNK)
    xspec = pl.BlockSpec((None, ts, d), lambda b, j: (b, j, 0))
    return pl.pallas_call(
        kern,
        grid=(bsz, s // ts),
        in_specs=[
            xspec,
            _resident((1, d)),
            _resident((d, key)), _resident((d, key)), _resident((d, val)), _resident((d, val)),
            _resident((d, LANES)), _resident((LANES, key)), _resident((1, key)),
            _resident((1, val)), _resident((val, d)),
        ],
        out_specs=xspec,
        out_shape=jax.ShapeDtypeStruct(x.shape, F32),
        scratch_shapes=[
            pltpu.VMEM((ts, d), BF16),
            pltpu.VMEM((ts, key), F32),
            pltpu.VMEM((ts, key), F32),
            pltpu.VMEM((ts, key), F32),
            pltpu.VMEM((ts, val), BF16),
            pltpu.VMEM((ts, val), F32),
            pltpu.VMEM((ts, val), BF16),
            pltpu.VMEM((2 * ts, key), BF16),
            pltpu.VMEM((ts // CHUNK * GLA_HEADS * GLA_DK, GLA_DK), F32),
            pltpu.VMEM((GLA_HEADS, GLA_DK, GLA_DV), F32),
        ],
        compiler_params=pltpu.CompilerParams(
            dimension_semantics=("arbitrary", "arbitrary"), vmem_limit_bytes=VMEM_LIMIT_BYTES),
        name="gla_layer",
    )(x, nw.reshape(1, d), wq, wk, wv, wr, wz, wgk, b_gk.reshape(1, key),
      jnp.tile(gnorm_w, GLA_HEADS).reshape(1, val), w_out.astype(BF16))


def _gdn_kernel(x_ref, nw_ref, wqkv_ref, wr_ref, wa_ref, wb_ref, cw_ref, alog_ref, dtb_ref,
                gw_ref, wo_ref, o_ref,
                h_ref, q_ref, k_ref, v_ref, g_ref, beta_ref, u_ref, wq_ref, kendt_ref, qk_ref,
                elast_ref, tail_ref, state_ref, *, n_chunks, solve_group):
    ts = x_ref.shape[0]
    key = GDN_HEADS * GDN_DK

    @pl.when(pl.program_id(1) == 0)
    def _():
        state_ref[...] = jnp.zeros_like(state_ref)
        tail_ref[...] = jnp.zeros_like(tail_ref)

    h = _rmsnorm(x_ref[...], nw_ref[...]).astype(BF16)
    h_ref[...] = h

    pre = jnp.dot(h, wqkv_ref[...], preferred_element_type=F32)
    ext = jnp.concatenate([tail_ref[...], pre], axis=0)
    tail_ref[...] = pre[ts - SUBLANES:, :]
    cw = cw_ref[...]
    conv = pre * cw[CONV_K - 1:CONV_K, :]
    for back in range(1, CONV_K):
        shifted = pltpu.roll(ext, back, axis=0)[SUBLANES:, :]
        conv = conv + shifted * cw[CONV_K - 1 - back:CONV_K - back, :]
    qkv = conv * jax.nn.sigmoid(conv)

    scale = GDN_DK ** -0.5
    for hd in range(GDN_HEADS):
        ql = slice(hd * GDN_DK, (hd + 1) * GDN_DK)
        kl = slice(key + hd * GDN_DK, key + (hd + 1) * GDN_DK)
        qh = qkv[:, ql]
        kh = qkv[:, kl]
        q_ref[:, ql] = qh * lax.rsqrt(jnp.sum(qh * qh, axis=-1, keepdims=True) + EPS) * scale
        k_ref[:, ql] = kh * lax.rsqrt(jnp.sum(kh * kh, axis=-1, keepdims=True) + EPS)
    v_ref[...] = qkv[:, 2 * key:]

    a = jnp.dot(h, wa_ref[...], preferred_element_type=F32)
    bl = jnp.dot(h, wb_ref[...], preferred_element_type=F32)
    beta_ref[...] = jax.nn.sigmoid(bl)
    g_ref[...] = -jnp.exp(alog_ref[...]) * _softplus(a + dtb_ref[...])

    row = lax.broadcasted_iota(jnp.int32, (CHUNK, LANES), 0)
    lane = lax.broadcasted_iota(jnp.int32, (CHUNK, LANES), 1)
    causal = row >= lane
    strict = row > lane
    upper = lane >= CHUNK
    eye_upper = jnp.where(lane == row + CHUNK, 1.0, 0.0)
    zeros_c = jnp.zeros((CHUNK, LANES), F32)
    zeros_rhs = jnp.zeros((CHUNK, GDN_DV + GDN_DK), F32)
    heads = range(GDN_HEADS)
    hls = [slice(hd * GDN_DK, (hd + 1) * GDN_DK) for hd in heads]

    def solve_body(grp, carry):
        chains = [(j, hd) for j in range(solve_group) for hd in heads]
        rows, rows2, dcum, dcum_t, beta, e_d, e_end = [], [], [], [], [], [], []
        for j in range(solve_group):
            c = grp * solve_group + j
            rows.append(pl.ds(pl.multiple_of(c * CHUNK, CHUNK), CHUNK))
            rows2.append(pl.ds(pl.multiple_of(c * 2 * CHUNK, 2 * CHUNK), 2 * CHUNK))
            dcum.append(_cumsum_rows(g_ref[rows[j], :]))
            dcum_t.append(jnp.concatenate([dcum[j], zeros_c], axis=0).T)
            beta.append(beta_ref[rows[j], :])
            e_d.append(jnp.exp(dcum[j]))
            d_last = dcum[j][CHUNK - 1:CHUNK, :]
            e_end.append(jnp.exp(d_last - dcum[j]))
            elast_ref[pl.ds(pl.multiple_of(c * SUBLANES, SUBLANES), SUBLANES), :] = jnp.broadcast_to(
                jnp.exp(d_last), (SUBLANES, LANES))
        ks = [k_ref[rows[j], hls[hd]] for j, hd in chains]
        qs = [q_ref[rows[j], hls[hd]] for j, hd in chains]
        kbs = [ks[i] * beta[j][:, hd:hd + 1] for i, (j, hd) in enumerate(chains)]
        grams = [_dot_nt(jnp.concatenate([kbs[i], qs[i]], axis=0),
                         jnp.concatenate([ks[i], zeros_c], axis=0)) for i in range(len(chains))]
        ws = []
        for i, (j, hd) in enumerate(chains):
            decay = jnp.exp(jnp.minimum(dcum[j][:, hd:hd + 1] - dcum_t[j][hd:hd + 1, :], 0.0))
            qk_ref[rows[j], hls[hd]] = jnp.where(causal, grams[i][CHUNK:] * decay, 0.0).astype(BF16)
            ws.append(jnp.where(strict, -(grams[i][:CHUNK] * decay), 0.0) + eye_upper)
        span = 1
        while span < CHUNK:
            prods = [_dot(w, jnp.concatenate([w, zeros_c], axis=0)) for w in ws]
            ws = [p + jnp.where(upper, w, 0.0) for p, w in zip(prods, ws)]
            span *= 2
        for i, (j, hd) in enumerate(chains):
            rhs = jnp.concatenate([v_ref[rows[j], hls[hd]] * beta[j][:, hd:hd + 1],
                                   kbs[i] * e_d[j][:, hd:hd + 1]], axis=-1)
            sol = _dot(ws[i], jnp.concatenate([zeros_rhs, rhs], axis=0))
            u_ref[rows[j], hls[hd]] = sol[:, :GDN_DV]
            wq_ref[rows2[j], hls[hd]] = jnp.concatenate(
                [sol[:, GDN_DV:], qs[i] * e_d[j][:, hd:hd + 1]], axis=0).astype(BF16)
            kend_t = (ks[i] * e_end[j][:, hd:hd + 1]).T
            kendt_ref[rows2[j], hd * LANES:hd * LANES + CHUNK] = kend_t.astype(BF16)
        return carry

    lax.fori_loop(0, n_chunks // solve_group, solve_body, 0)

    def scan_body(c, carry):
        rows = pl.ds(pl.multiple_of(c * CHUNK, CHUNK), CHUNK)
        rows2 = pl.ds(pl.multiple_of(c * 2 * CHUNK, 2 * CHUNK), 2 * CHUNK)
        e_last = elast_ref[pl.ds(pl.multiple_of(c * SUBLANES, SUBLANES), SUBLANES), :][0:1, :]
        sts = [state_ref[hd] for hd in heads]
        wqs = [_dot(wq_ref[rows2, hls[hd]], sts[hd]) for hd in heads]
        v_news = [u_ref[rows, hls[hd]] - wqs[hd][:CHUNK] for hd in heads]
        for hd in heads:
            qk = qk_ref[rows, hd * LANES:hd * LANES + CHUNK]
            u_ref[rows, hls[hd]] = wqs[hd][CHUNK:] + _dot(qk, v_news[hd])
        for hd in heads:
            kend_t = kendt_ref[rows2, hd * LANES:hd * LANES + CHUNK]
            state_ref[hd] = sts[hd] * e_last[:, hd:hd + 1] + _dot(kend_t, v_news[hd])
        return carry

    lax.fori_loop(0, n_chunks, scan_body, 0)
    att_ref = u_ref

    o = att_ref[...]
    parts = []
    for hd in range(GDN_HEADS):
        oh = o[:, hd * GDN_DV:(hd + 1) * GDN_DV]
        parts.append(oh * lax.rsqrt(jnp.mean(oh * oh, axis=-1, keepdims=True) + EPS))
    on = jnp.concatenate(parts, axis=-1) * gw_ref[...]
    r = jnp.dot(h_ref[...], wr_ref[...], preferred_element_type=F32)
    y = on * (r * jax.nn.sigmoid(r))
    o_ref[...] = x_ref[...] + jnp.dot(y.astype(BF16), wo_ref[...], preferred_element_type=F32)


def _gdn_layer(x, nw, w_in, conv_w, a_log, dt_bias, gnorm_w, w_out, *, ts):
    bsz, s, d = x.shape
    key = GDN_HEADS * GDN_DK
    val = GDN_HEADS * GDN_DV
    conv_ch = 2 * key + val
    assert s % ts == 0 and ts % CHUNK == 0
    w = w_in.astype(BF16)
    wqkv, wr = w[:, :conv_ch], w[:, conv_ch:conv_ch + val]
    pad = ((0, 0), (0, LANES - GDN_HEADS))
    wa = jnp.pad(w[:, conv_ch + val:conv_ch + val + GDN_HEADS], pad)
    wb = jnp.pad(w[:, conv_ch + val + GDN_HEADS:], pad)
    alog = jnp.pad(a_log.reshape(1, GDN_HEADS), pad)
    dtb = jnp.pad(dt_bias.reshape(1, GDN_HEADS), pad)
    n_chunks = ts // CHUNK
    solve_group = 4 if n_chunks % 4 == 0 else 1
    kern = functools.partial(_gdn_kernel, n_chunks=n_chunks, solve_group=solve_group)
    xspec = pl.BlockSpec((None, ts, d), lambda b, j: (b, j, 0))
    return pl.pallas_call(
        kern,
        grid=(bsz, s // ts),
        in_specs=[
            xspec,
            _resident((1, d)),
            _resident((d, conv_ch)), _resident((d, val)), _resident((d, LANES)), _resident((d, LANES)),
            _resident((CONV_K, conv_ch)), _resident((1, LANES)), _resident((1, LANES)),
            _resident((1, val)), _resident((val, d)),
        ],
        out_specs=xspec,
        out_shape=jax.ShapeDtypeStruct(x.shape, F32),
        scratch_shapes=[
            pltpu.VMEM((ts, d), BF16),
            pltpu.VMEM((ts, key), F32),
            pltpu.VMEM((ts, key), F32),
            pltpu.VMEM((ts, val), F32),
            pltpu.VMEM((ts, LANES), F32),
            pltpu.VMEM((ts, LANES), F32),
            pltpu.VMEM((ts, val), F32),
            pltpu.VMEM((2 * ts, key), BF16),
            pltpu.VMEM((2 * ts, key), BF16),
            pltpu.VMEM((ts, key), BF16),
            pltpu.VMEM((ts // CHUNK * SUBLANES, LANES), F32),
            pltpu.VMEM((SUBLANES, conv_ch), F32),
            pltpu.VMEM((GDN_HEADS, GDN_DK, GDN_DV), F32),
        ],
        compiler_params=pltpu.CompilerParams(
            dimension_semantics=("arbitrary", "arbitrary"), vmem_limit_bytes=VMEM_LIMIT_BYTES),
        name="gdn_layer",
    )(x, nw.reshape(1, d), wqkv, wr, wa, wb, conv_w, alog, dtb,
      jnp.tile(gnorm_w, GDN_HEADS).reshape(1, val), w_out.astype(BF16))


def _trunk(x, norm_w, ffn_w_in, ffn_w_out, gla_w_in, gla_w_gk, gla_b_gk, gla_norm_w, gla_w_out,
           gdn_w_in, gdn_conv_w, gdn_a_log, gdn_dt_bias, gdn_norm_w, gdn_w_out, final_norm_w,
           *, tm, ts):
    bsz, s, d = x.shape
    depth = norm_w.shape[0]
    for i in range(depth):
        x = _ffn(x.reshape(bsz * s, d), norm_w[i, 0], ffn_w_in[i, 0], ffn_w_out[i, 0],
                 final_norm_w, tm=tm, final_norm=False).reshape(bsz, s, d)
        j = i // 2
        if i % 2 == 0:
            x = _gla_layer(x, norm_w[i, 1], gla_w_in[j], gla_w_gk[j], gla_b_gk[j], gla_norm_w[j],
                           gla_w_out[j], ts=ts)
        else:
            x = _gdn_layer(x, norm_w[i, 1], gdn_w_in[j], gdn_conv_w[j], gdn_a_log[j],
                           gdn_dt_bias[j], gdn_norm_w[j], gdn_w_out[j], ts=ts)
        x = _ffn(x.reshape(bsz * s, d), norm_w[i, 2], ffn_w_in[i, 1], ffn_w_out[i, 1],
                 final_norm_w, tm=tm, final_norm=(i == depth - 1)).reshape(bsz, s, d)
    return x


def kernel(x, norm_w, ffn_w_in, ffn_w_out, gla_w_in, gla_w_gk, gla_b_gk, gla_norm_w, gla_w_out,
           gdn_w_in, gdn_conv_w, gdn_a_log, gdn_dt_bias, gdn_norm_w, gdn_w_out, final_norm_w):
    return _trunk(x, norm_w, ffn_w_in, ffn_w_out, gla_w_in, gla_w_gk, gla_b_gk, gla_norm_w,
                  gla_w_out, gdn_w_in, gdn_conv_w, gdn_a_log, gdn_dt_bias, gdn_norm_w, gdn_w_out,
                  final_norm_w, tm=1024, ts=512)
```

```python
import functools

import jax
import jax.numpy as jnp
from jax import lax
from jax.experimental import pallas as pl
from jax.experimental.pallas import tpu as pltpu

F32 = jnp.float32
BF16 = jnp.bfloat16

EPS = 1e-6
CHUNK = 64
LANES = 128
SUBLANES = 8
MXU_N = 256

GLA_HEADS, GLA_DK, GLA_DV = 4, 128, 256
GLA_GATE_NORM = 16.0
GDN_HEADS, GDN_DK, GDN_DV = 8, 128, 128
CONV_K = 4
CONV_PHASES = 4

VMEM_LIMIT_BYTES = 60000 * 1024


def _rmsnorm(x, w):
    return x * lax.rsqrt(jnp.mean(x * x, axis=-1, keepdims=True) + EPS) * w


def _dot(a, b):
    return jnp.dot(a.astype(BF16), b.astype(BF16), preferred_element_type=F32)


def _dot_nt(a, b):
    return lax.dot_general(a.astype(BF16), b.astype(BF16), (((1,), (1,)), ((), ())),
                           preferred_element_type=F32)


def _dot_tn(a, b):
    return jnp.dot(a.astype(F32).T.astype(BF16), b.astype(BF16), preferred_element_type=F32)


def _softplus(x):
    return jnp.maximum(x, 0.0) + jnp.log1p(jnp.exp(-jnp.abs(x)))


def _rows(index, size):
    start = index * size
    return pl.ds(start if isinstance(start, int) else pl.multiple_of(start, size), size)


def _run(stages):
    for stage in stages:
        stage()


def _interleave(major, minor):
    out = []
    for i, stage in enumerate(major):
        out.append(stage)
        out.extend(m for k, m in enumerate(minor) if (k * len(major)) // len(minor) == i)
    return out


def _cumsum_rows(x):
    rows = x.shape[0]
    row = lax.broadcasted_iota(jnp.int32, x.shape, 0)
    shift = 1
    while shift < rows:
        x = x + jnp.where(row >= shift, pltpu.roll(x, shift, axis=0), 0.0)
        shift *= 2
    return x


def _ffn_kernel(x_ref, nw_ref, wi_ref, wo_ref, fw_ref, o_ref, h_ref, acc_ref, *, final_norm):
    f = wo_ref.shape[0]
    h_ref[...] = _rmsnorm(x_ref[...], nw_ref[...]).astype(BF16)
    acc_ref[...] = jnp.zeros_like(acc_ref)
    for lo in range(0, f, MXU_N):
        h = h_ref[...]
        g = jnp.dot(h, wi_ref[:, lo:lo + MXU_N], preferred_element_type=F32)
        u = jnp.dot(h, wi_ref[:, f + lo:f + lo + MXU_N], preferred_element_type=F32)
        a = (g * jax.nn.sigmoid(g) * u).astype(BF16)
        acc_ref[...] += jnp.dot(a, wo_ref[lo:lo + MXU_N, :], preferred_element_type=F32)
    y = x_ref[...] + 0.5 * acc_ref[...]
    if final_norm:
        y = _rmsnorm(y, fw_ref[...])
    o_ref[...] = y


def _resident(shape):
    nd = len(shape)
    return pl.BlockSpec(shape, lambda *_: (0,) * nd, pipeline_mode=pl.Buffered(1))


def _ffn(x2d, nw, w_in, w_out, final_w, *, tm, final_norm):
    m, d = x2d.shape
    f = w_out.shape[0]
    assert f % MXU_N == 0 and m % tm == 0
    kern = functools.partial(_ffn_kernel, final_norm=final_norm)
    return pl.pallas_call(
        kern,
        grid=(m // tm,),
        in_specs=[
            pl.BlockSpec((tm, d), lambda i: (i, 0)),
            _resident((1, d)),
            _resident((d, 2 * f)),
            _resident((f, d)),
            _resident((1, d)),
        ],
        out_specs=pl.BlockSpec((tm, d), lambda i: (i, 0)),
        out_shape=jax.ShapeDtypeStruct((m, d), F32),
        scratch_shapes=[pltpu.VMEM((tm, d), BF16), pltpu.VMEM((tm, d), F32)],
        compiler_params=pltpu.CompilerParams(
            dimension_semantics=("arbitrary",), vmem_limit_bytes=VMEM_LIMIT_BYTES),
        name="ffn",
    )(x2d, nw.reshape(1, d), w_in.astype(BF16), w_out.astype(BF16), final_w.reshape(1, d))


def _gla_kernel(x_ref, nw_ref, wq_ref, wk_ref, wv_ref, wr_ref, wz_ref, wgk_ref, bgk_ref,
                gw_ref, wo_ref, o_ref,
                h_ref, q_ref, k_ref, g_ref, v_ref, att_ref, state_ref, *, n_chunks, group):
    @pl.when(pl.program_id(1) == 0)
    def _():
        state_ref[...] = jnp.zeros_like(state_ref)

    h = _rmsnorm(x_ref[...], nw_ref[...]).astype(BF16)
    h_ref[...] = h
    q_ref[...] = jnp.dot(h, wq_ref[...], preferred_element_type=F32)
    k_ref[...] = jnp.dot(h, wk_ref[...], preferred_element_type=F32)
    v_ref[...] = jnp.dot(h, wv_ref[...], preferred_element_type=F32).astype(BF16)
    z = jnp.dot(h, wz_ref[...], preferred_element_type=F32)
    g_ref[...] = -_softplus(-(_dot(z, wgk_ref[...]) + bgk_ref[...])) / GLA_GATE_NORM

    scale = GLA_DK ** -0.5
    causal = (lax.broadcasted_iota(jnp.int32, (CHUNK, LANES), 0)
              >= lax.broadcasted_iota(jnp.int32, (CHUNK, LANES), 1))
    zeros_c = jnp.zeros((CHUNK, LANES), F32)
    zeros_v = jnp.zeros((CHUNK, GLA_DV), BF16)
    heads = range(GLA_HEADS)
    kls = [slice(hd * GLA_DK, (hd + 1) * GLA_DK) for hd in heads]
    vls = [slice(hd * GLA_DV, (hd + 1) * GLA_DV) for hd in heads]

    def group_body(grp, carry):
        prep = []
        for j in range(group):
            rows = _rows(grp * group + j, CHUNK)
            b = _cumsum_rows(g_ref[rows, :])
            b_last = b[CHUNK - 1:CHUNK, :]
            q = q_ref[rows, :]
            k = k_ref[rows, :]
            q_dec = q * jnp.exp(b) * scale
            k_inv = k * jnp.exp(-b)
            k_end = k * jnp.exp(b_last - b)
            decay = jnp.exp(b_last)
            vs = [v_ref[rows, vl] for vl in vls]
            scores = [_dot_nt(q_dec[:, kl], jnp.concatenate([k_inv[:, kl], zeros_c], axis=0))
                      for kl in kls]
            kvs = [jnp.dot(k_end[:, kls[hd]].T.astype(BF16), vs[hd], preferred_element_type=F32)
                   for hd in heads]
            decs = [jnp.broadcast_to(decay[:, kl], (GLA_DK, GLA_DK)).T for kl in kls]
            prep.append((rows, q_dec, vs, scores, kvs, decs))
        sts = [state_ref[hd] for hd in heads]
        for rows, q_dec, vs, scores, kvs, decs in prep:
            for hd in heads:
                a = jnp.where(causal, scores[hd], 0.0)
                qa = jnp.concatenate([q_dec[:, kls[hd]], a], axis=-1).astype(BF16)
                rhs = jnp.concatenate([sts[hd].astype(BF16), vs[hd], zeros_v], axis=0)
                att_ref[rows, vls[hd]] = jnp.dot(qa, rhs, preferred_element_type=F32)
            sts = [sts[hd] * jnp.concatenate([decs[hd], decs[hd]], axis=-1) + kvs[hd] for hd in heads]
        for hd in heads:
            state_ref[hd] = sts[hd]
        return carry

    lax.fori_loop(0, n_chunks // group, group_body, 0)

    o = att_ref[...]
    parts = []
    for hd in range(GLA_HEADS):
        oh = o[:, hd * GLA_DV:(hd + 1) * GLA_DV]
        parts.append(oh * lax.rsqrt(jnp.mean(oh * oh, axis=-1, keepdims=True) + EPS))
    on = jnp.concatenate(parts, axis=-1) * gw_ref[...]
    r = jnp.dot(h_ref[...], wr_ref[...], preferred_element_type=F32)
    y = on * (r * jax.nn.sigmoid(r))
    o_ref[...] = x_ref[...] + jnp.dot(y.astype(BF16), wo_ref[...], preferred_element_type=F32)


def _gla_layer(x, nw, w_in, w_gk, b_gk, gnorm_w, w_out, *, ts):
    bsz, s, d = x.shape
    key = GLA_HEADS * GLA_DK
    val = GLA_HEADS * GLA_DV
    rank = w_gk.shape[0]
    assert s % ts == 0 and ts % CHUNK == 0
    w = w_in.astype(BF16)
    wq, wk = w[:, :key], w[:, key:2 * key]
    wv, wr = w[:, 2 * key:2 * key + val], w[:, 2 * key + val:2 * key + 2 * val]
    wz = jnp.pad(w[:, 2 * key + 2 * val:], ((0, 0), (0, LANES - rank)))
    wgk = jnp.pad(w_gk.astype(BF16), ((0, LANES - rank), (0, 0)))
    n_chunks = ts // CHUNK
    group = 2
    assert n_chunks % group == 0
    kern = functools.partial(_gla_kernel, n_chunks=n_chunks, group=group)
    xspec = pl.BlockSpec((None, ts, d), lambda b, j: (b, j, 0))
    return pl.pallas_call(
        kern,
        grid=(bsz, s // ts),
        in_specs=[
            xspec,
            _resident((1, d)),
            _resident((d, key)), _resident((d, key)), _resident((d, val)), _resident((d, val)),
            _resident((d, LANES)), _resident((LANES, key)), _resident((1, key)),
            _resident((1, val)), _resident((val, d)),
        ],
        out_specs=xspec,
        out_shape=jax.ShapeDtypeStruct(x.shape, F32),
        scratch_shapes=[
            pltpu.VMEM((ts, d), BF16),
            pltpu.VMEM((ts, key), F32),
            pltpu.VMEM((ts, key), F32),
            pltpu.VMEM((ts, key), F32),
            pltpu.VMEM((ts, val), BF16),
            pltpu.VMEM((ts, val), F32),
            pltpu.VMEM((GLA_HEADS, GLA_DK, GLA_DV), F32),
        ],
        compiler_params=pltpu.CompilerParams(
            dimension_semantics=("arbitrary", "arbitrary"), vmem_limit_bytes=VMEM_LIMIT_BYTES),
        name="gla_layer",
    )(x, nw.reshape(1, d), wq, wk, wv, wr, wz, wgk, b_gk.reshape(1, key),
      jnp.tile(gnorm_w, GLA_HEADS).reshape(1, val), w_out.astype(BF16))


def _gdn_kernel(x_ref, nw_ref, wqkv_ref, wr_ref, wa_ref, wb_ref, cw_ref, alog_ref, dtb_ref,
                gw_ref, wo_ref, o_ref,
                h_ref, q_ref, k_ref, v_ref, g_ref, beta_ref, u_ref, wq_ref, kendt_ref, qk_ref,
                elast_ref, ext_ref, state_ref, *, n_chunks, group):
    ts = x_ref.shape[0]
    n_slabs = ext_ref.shape[0]

    @pl.when(pl.program_id(1) == 0)
    def _():
        state_ref[...] = jnp.zeros_like(state_ref)
        ext_ref[:, 0:SUBLANES, :] = jnp.zeros((n_slabs, SUBLANES, LANES), F32)

    @pl.when(pl.program_id(1) != 0)
    def _():
        ext_ref[:, 0:SUBLANES, :] = ext_ref[:, ts:ts + SUBLANES, :]

    h = _rmsnorm(x_ref[...], nw_ref[...]).astype(BF16)
    h_ref[...] = h

    pre = jnp.dot(h, wqkv_ref[...], preferred_element_type=F32)
    for s in range(n_slabs):
        ext_ref[s, SUBLANES:, :] = pre[:, s * LANES:(s + 1) * LANES]

    scale = GDN_DK ** -0.5
    rows_ph = ts // CONV_PHASES
    for s in range(n_slabs):
        cw = cw_ref[:, s * LANES:(s + 1) * LANES]
        first = SUBLANES - (CONV_K - 1)
        taps = {start: ext_ref[s, pl.ds(start, rows_ph, stride=CONV_PHASES), :]
                for start in range(first, SUBLANES + CONV_PHASES)}
        for r in range(CONV_PHASES):
            conv = taps[SUBLANES + r] * cw[CONV_K - 1:CONV_K, :]
            for back in range(1, CONV_K):
                conv = conv + taps[SUBLANES + r - back] * cw[CONV_K - 1 - back:CONV_K - back, :]
            y = conv * jax.nn.sigmoid(conv)
            dst = pl.ds(r, rows_ph, stride=CONV_PHASES)
            if s < GDN_HEADS:
                y = y * lax.rsqrt(jnp.sum(y * y, axis=-1, keepdims=True) + EPS) * scale
                q_ref[s, dst, :] = y
            elif s < 2 * GDN_HEADS:
                y = y * lax.rsqrt(jnp.sum(y * y, axis=-1, keepdims=True) + EPS)
                k_ref[s - GDN_HEADS, dst, :] = y
            else:
                v_ref[s - 2 * GDN_HEADS, dst, :] = y

    a = jnp.dot(h, wa_ref[...], preferred_element_type=F32)
    bl = jnp.dot(h, wb_ref[...], preferred_element_type=F32)
    beta_ref[...] = jax.nn.sigmoid(bl)
    g_ref[...] = -jnp.exp(alog_ref[...]) * _softplus(a + dtb_ref[...])

    row = lax.broadcasted_iota(jnp.int32, (CHUNK, LANES), 0)
    lane = lax.broadcasted_iota(jnp.int32, (CHUNK, LANES), 1)
    causal = row >= lane
    strict = row > lane
    upper = lane >= CHUNK
    eye_upper = jnp.where(lane == row + CHUNK, 1.0, 0.0)
    zeros_c = jnp.zeros((CHUNK, LANES), F32)
    zeros_rhs = jnp.zeros((CHUNK, GDN_DV + GDN_DK), F32)
    heads = range(GDN_HEADS)
    hls = [slice(hd * GDN_DK, (hd + 1) * GDN_DK) for hd in heads]

    def solve_stages(chunk_ids):
        chains = [(j, hd) for j in range(len(chunk_ids)) for hd in heads]
        t = {}

        def grams():
            rows = [_rows(c, CHUNK) for c in chunk_ids]
            rows2 = [_rows(c, 2 * CHUNK) for c in chunk_ids]
            dcum, dcum_t, beta, e_d, e_end = [], [], [], [], []
            for j, c in enumerate(chunk_ids):
                dcum.append(_cumsum_rows(g_ref[rows[j], :]))
                dcum_t.append(jnp.concatenate([dcum[j], zeros_c], axis=0).T)
                beta.append(beta_ref[rows[j], :])
                e_d.append(jnp.exp(dcum[j]))
                d_last = dcum[j][CHUNK - 1:CHUNK, :]
                e_end.append(jnp.exp(d_last - dcum[j]))
                elast_ref[_rows(c, SUBLANES), :] = jnp.broadcast_to(jnp.exp(d_last), (SUBLANES, LANES))
            ks = [k_ref[hd, rows[j], :] for j, hd in chains]
            qs = [q_ref[hd, rows[j], :] for j, hd in chains]
            kbs = [ks[i] * beta[j][:, hd:hd + 1] for i, (j, hd) in enumerate(chains)]
            t["grams"] = [_dot_nt(jnp.concatenate([kbs[i], qs[i]], axis=0),
                                  jnp.concatenate([ks[i], zeros_c], axis=0)) for i in range(len(chains))]
            t.update(rows=rows, rows2=rows2, dcum=dcum, dcum_t=dcum_t, beta=beta, e_d=e_d, e_end=e_end,
                     ks=ks, qs=qs, kbs=kbs)

        def double():
            prods = [_dot(w, jnp.concatenate([w, zeros_c], axis=0)) for w in t["ws"]]
            t["ws"] = [p + jnp.where(upper, w, 0.0) for p, w in zip(prods, t["ws"])]

        def mask_and_double():
            ws = []
            for i, (j, hd) in enumerate(chains):
                decay = jnp.exp(jnp.minimum(
                    t["dcum"][j][:, hd:hd + 1] - t["dcum_t"][j][hd:hd + 1, :], 0.0))
                gram = t["grams"][i]
                qk_ref[t["rows"][j], hls[hd]] = jnp.where(causal, gram[CHUNK:] * decay, 0.0).astype(BF16)
                ws.append(jnp.where(strict, -(gram[:CHUNK] * decay), 0.0) + eye_upper)
            t["ws"] = ws
            double()

        def apply():
            for i, (j, hd) in enumerate(chains):
                rows, rows2, e_d = t["rows"][j], t["rows2"][j], t["e_d"][j]
                rhs = jnp.concatenate([v_ref[hd, rows, :] * t["beta"][j][:, hd:hd + 1],
                                       t["kbs"][i] * e_d[:, hd:hd + 1]], axis=-1)
                sol = _dot(t["ws"][i], jnp.concatenate([zeros_rhs, rhs], axis=0))
                u_ref[rows, hls[hd]] = sol[:, :GDN_DV]
                wq_ref[rows2, hls[hd]] = jnp.concatenate(
                    [sol[:, GDN_DV:], t["qs"][i] * e_d[:, hd:hd + 1]], axis=0).astype(BF16)
                kend_t = (t["ks"][i] * t["e_end"][j][:, hd:hd + 1]).T
                kendt_ref[rows2, hd * LANES:hd * LANES + CHUNK] = kend_t.astype(BF16)

        n_double = CHUNK.bit_length() - 1
        return [grams, mask_and_double] + [double] * (n_double - 1) + [apply]

    def scan_stages(chunk_ids):
        t = {}

        def project(c, first):
            if first:
                t["sts"] = [state_ref[hd] for hd in heads]
            rows2 = _rows(c, 2 * CHUNK)
            t["wqs"] = [_dot(wq_ref[rows2, hls[hd]], t["sts"][hd]) for hd in heads]

        def update(c, last):
            rows, rows2 = _rows(c, CHUNK), _rows(c, 2 * CHUNK)
            e_last = elast_ref[_rows(c, SUBLANES), :][0:1, :]
            v_news = [u_ref[rows, hls[hd]] - t["wqs"][hd][:CHUNK] for hd in heads]
            for hd in heads:
                qk = qk_ref[rows, hd * LANES:hd * LANES + CHUNK]
                u_ref[rows, hls[hd]] = t["wqs"][hd][CHUNK:] + _dot(qk, v_news[hd])
            sts = []
            for hd in heads:
                kend_t = kendt_ref[rows2, hd * LANES:hd * LANES + CHUNK]
                sts.append(t["sts"][hd] * e_last[:, hd:hd + 1] + _dot(kend_t, v_news[hd]))
            t["sts"] = sts
            if last:
                for hd in heads:
                    state_ref[hd] = sts[hd]

        stages = []
        for idx, c in enumerate(chunk_ids):
            stages.append(functools.partial(project, c, idx == 0))
            stages.append(functools.partial(update, c, idx == len(chunk_ids) - 1))
        return stages

    n_groups = n_chunks // group
    group_ids = lambda g: [g * group + j for j in range(group)]
    _run(solve_stages(group_ids(0)))

    def pipelined_body(g, carry):
        _run(_interleave(solve_stages(group_ids(g + 1)), scan_stages(group_ids(g))))
        return carry

    lax.fori_loop(0, n_groups - 1, pipelined_body, 0)
    _run(scan_stages(group_ids(n_groups - 1)))
    att_ref = u_ref

    o = att_ref[...]
    parts = []
    for hd in range(GDN_HEADS):
        oh = o[:, hd * GDN_DV:(hd + 1) * GDN_DV]
        parts.append(oh * lax.rsqrt(jnp.mean(oh * oh, axis=-1, keepdims=True) + EPS))
    on = jnp.concatenate(parts, axis=-1) * gw_ref[...]
    r = jnp.dot(h_ref[...], wr_ref[...], preferred_element_type=F32)
    y = on * (r * jax.nn.sigmoid(r))
    o_ref[...] = x_ref[...] + jnp.dot(y.astype(BF16), wo_ref[...], preferred_element_type=F32)


def _gdn_layer(x, nw, w_in, conv_w, a_log, dt_bias, gnorm_w, w_out, *, ts):
    bsz, s, d = x.shape
    key = GDN_HEADS * GDN_DK
    val = GDN_HEADS * GDN_DV
    conv_ch = 2 * key + val
    assert s % ts == 0 and ts % CHUNK == 0
    w = w_in.astype(BF16)
    wqkv, wr = w[:, :conv_ch], w[:, conv_ch:conv_ch + val]
    pad = ((0, 0), (0, LANES - GDN_HEADS))
    wa = jnp.pad(w[:, conv_ch + val:conv_ch + val + GDN_HEADS], pad)
    wb = jnp.pad(w[:, conv_ch + val + GDN_HEADS:], pad)
    alog = jnp.pad(a_log.reshape(1, GDN_HEADS), pad)
    dtb = jnp.pad(dt_bias.reshape(1, GDN_HEADS), pad)
    n_chunks = ts // CHUNK
    group = 2
    assert n_chunks % group == 0
    kern = functools.partial(_gdn_kernel, n_chunks=n_chunks, group=group)
    xspec = pl.BlockSpec((None, ts, d), lambda b, j: (b, j, 0))
    return pl.pallas_call(
        kern,
        grid=(bsz, s // ts),
        in_specs=[
            xspec,
            _resident((1, d)),
            _resident((d, conv_ch)), _resident((d, val)), _resident((d, LANES)), _resident((d, LANES)),
            _resident((CONV_K, conv_ch)), _resident((1, LANES)), _resident((1, LANES)),
            _resident((1, val)), _resident((val, d)),
        ],
        out_specs=xspec,
        out_shape=jax.ShapeDtypeStruct(x.shape, F32),
        scratch_shapes=[
            pltpu.VMEM((ts, d), BF16),
            pltpu.VMEM((GDN_HEADS, ts, GDN_DK), F32),
            pltpu.VMEM((GDN_HEADS, ts, GDN_DK), F32),
            pltpu.VMEM((GDN_HEADS, ts, GDN_DV), F32),
            pltpu.VMEM((ts, LANES), F32),
            pltpu.VMEM((ts, LANES), F32),
            pltpu.VMEM((ts, val), F32),
            pltpu.VMEM((2 * ts, key), BF16),
            pltpu.VMEM((2 * ts, key), BF16),
            pltpu.VMEM((ts, key), BF16),
            pltpu.VMEM((ts // CHUNK * SUBLANES, LANES), F32),
            pltpu.VMEM((conv_ch // LANES, ts + SUBLANES, LANES), F32),
            pltpu.VMEM((GDN_HEADS, GDN_DK, GDN_DV), F32),
        ],
        compiler_params=pltpu.CompilerParams(
            dimension_semantics=("arbitrary", "arbitrary"), vmem_limit_bytes=VMEM_LIMIT_BYTES),
        name="gdn_layer",
    )(x, nw.reshape(1, d), wqkv, wr, wa, wb, conv_w, alog, dtb,
      jnp.tile(gnorm_w, GDN_HEADS).reshape(1, val), w_out.astype(BF16))


def _trunk(x, norm_w, ffn_w_in, ffn_w_out, gla_w_in, gla_w_gk, gla_b_gk, gla_norm_w, gla_w_out,
           gdn_w_in, gdn_conv_w, gdn_a_log, gdn_dt_bias, gdn_norm_w, gdn_w_out, final_norm_w,
           *, tm, ts):
    bsz, s, d = x.shape
    depth = norm_w.shape[0]
    for i in range(depth):
        x = _ffn(x.reshape(bsz * s, d), norm_w[i, 0], ffn_w_in[i, 0], ffn_w_out[i, 0],
                 final_norm_w, tm=tm, final_norm=False).reshape(bsz, s, d)
        j = i // 2
        if i % 2 == 0:
            x = _gla_layer(x, norm_w[i, 1], gla_w_in[j], gla_w_gk[j], gla_b_gk[j], gla_norm_w[j],
                           gla_w_out[j], ts=ts)
        else:
            x = _gdn_layer(x, norm_w[i, 1], gdn_w_in[j], gdn_conv_w[j], gdn_a_log[j],
                           gdn_dt_bias[j], gdn_norm_w[j], gdn_w_out[j], ts=ts)
        x = _ffn(x.reshape(bsz * s, d), norm_w[i, 2], ffn_w_in[i, 1], ffn_w_out[i, 1],
                 final_norm_w, tm=tm, final_norm=(i == depth - 1)).reshape(bsz, s, d)
    return x


def kernel(x, norm_w, ffn_w_in, ffn_w_out, gla_w_in, gla_w_gk, gla_b_gk, gla_norm_w, gla_w_out,
           gdn_w_in, gdn_conv_w, gdn_a_log, gdn_dt_bias, gdn_norm_w, gdn_w_out, final_norm_w):
    return _trunk(x, norm_w, ffn_w_in, ffn_w_out, gla_w_in, gla_w_gk, gla_b_gk, gla_norm_w,
                  gla_w_out, gdn_w_in, gdn_conv_w, gdn_a_log, gdn_dt_bias, gdn_norm_w, gdn_w_out,
                  final_norm_w, tm=1024, ts=512)
```

```python
import functools

import jax
import jax.numpy as jnp
from jax import lax
from jax.experimental import pallas as pl
from jax.experimental.pallas import tpu as pltpu

F32 = jnp.float32
BF16 = jnp.bfloat16

EPS = 1e-6
CHUNK = 64
LANES = 128
SUBLANES = 8
MXU_N = 256

GLA_HEADS, GLA_DK, GLA_DV = 4, 128, 256
GLA_GATE_NORM = 16.0
GDN_HEADS, GDN_DK, GDN_DV = 8, 128, 128
CONV_K = 4
CONV_PHASES = 4

VMEM_LIMIT_BYTES = 60000 * 1024

FFN_ROWS = 1024
GLA_ROWS = 1024
GDN_ROWS = 512


def _rmsnorm(x, w):
    return x * lax.rsqrt(jnp.mean(x * x, axis=-1, keepdims=True) + EPS) * w


def _dot(a, b):
    return jnp.dot(a.astype(BF16), b.astype(BF16), preferred_element_type=F32)


def _dot_nt(a, b):
    return lax.dot_general(a.astype(BF16), b.astype(BF16), (((1,), (1,)), ((), ())),
                           preferred_element_type=F32)


def _dot_tn(a, b):
    return jnp.dot(a.astype(F32).T.astype(BF16), b.astype(BF16), preferred_element_type=F32)


def _softplus(x):
    return jnp.maximum(x, 0.0) + jnp.log1p(jnp.exp(-jnp.abs(x)))


def _rows(index, size):
    start = index * size
    return pl.ds(start if isinstance(start, int) else pl.multiple_of(start, size), size)


def _run(stages):
    for stage in stages:
        stage()


def _interleave(major, minor):
    out = []
    for i, stage in enumerate(major):
        out.append(stage)
        out.extend(m for k, m in enumerate(minor) if (k * len(major)) // len(minor) == i)
    return out


def _cumsum_rows(x):
    rows = x.shape[0]
    row = lax.broadcasted_iota(jnp.int32, x.shape, 0)
    shift = 1
    while shift < rows:
        x = x + jnp.where(row >= shift, pltpu.roll(x, shift, axis=0), 0.0)
        shift *= 2
    return x


def _ffn_kernel(x_ref, nw_ref, wi_ref, wo_ref, fw_ref, o_ref, h_ref, acc_ref, *, final_norm):
    f = wo_ref.shape[0]
    h_ref[...] = _rmsnorm(x_ref[...], nw_ref[...]).astype(BF16)
    acc_ref[...] = jnp.zeros_like(acc_ref)
    for lo in range(0, f, MXU_N):
        h = h_ref[...]
        g = jnp.dot(h, wi_ref[:, lo:lo + MXU_N], preferred_element_type=F32)
        u = jnp.dot(h, wi_ref[:, f + lo:f + lo + MXU_N], preferred_element_type=F32)
        a = (g * jax.nn.sigmoid(g) * u).astype(BF16)
        acc_ref[...] += jnp.dot(a, wo_ref[lo:lo + MXU_N, :], preferred_element_type=F32)
    y = x_ref[...] + 0.5 * acc_ref[...]
    if final_norm:
        y = _rmsnorm(y, fw_ref[...])
    o_ref[...] = y


def _resident(shape):
    nd = len(shape)
    return pl.BlockSpec(shape, lambda *_: (0,) * nd, pipeline_mode=pl.Buffered(1))


def _ffn(x2d, nw, w_in, w_out, final_w, *, tm, final_norm):
    m, d = x2d.shape
    f = w_out.shape[0]
    assert f % MXU_N == 0 and m % tm == 0
    kern = functools.partial(_ffn_kernel, final_norm=final_norm)
    return pl.pallas_call(
        kern,
        grid=(m // tm,),
        in_specs=[
            pl.BlockSpec((tm, d), lambda i: (i, 0)),
            _resident((1, d)),
            _resident((d, 2 * f)),
            _resident((f, d)),
            _resident((1, d)),
        ],
        out_specs=pl.BlockSpec((tm, d), lambda i: (i, 0)),
        out_shape=jax.ShapeDtypeStruct((m, d), F32),
        scratch_shapes=[pltpu.VMEM((tm, d), BF16), pltpu.VMEM((tm, d), F32)],
        compiler_params=pltpu.CompilerParams(
            dimension_semantics=("arbitrary",), vmem_limit_bytes=VMEM_LIMIT_BYTES),
        name="ffn",
    )(x2d, nw.reshape(1, d), w_in.astype(BF16), w_out.astype(BF16), final_w.reshape(1, d))


def _gla_kernel(x_ref, nw_ref, wq_ref, wk_ref, wv_ref, wr_ref, wz_ref, wgk_ref, bgk_ref,
                gw_ref, wo_ref, o_ref,
                h_ref, q_ref, k_ref, g_ref, v_ref, att_ref, state_ref, *, n_chunks, group):
    @pl.when(pl.program_id(1) == 0)
    def _():
        state_ref[...] = jnp.zeros_like(state_ref)

    h = _rmsnorm(x_ref[...], nw_ref[...]).astype(BF16)
    h_ref[...] = h
    q_ref[...] = jnp.dot(h, wq_ref[...], preferred_element_type=F32)
    k_ref[...] = jnp.dot(h, wk_ref[...], preferred_element_type=F32)
    v_ref[...] = jnp.dot(h, wv_ref[...], preferred_element_type=F32).astype(BF16)
    z = jnp.dot(h, wz_ref[...], preferred_element_type=F32)
    g_ref[...] = -_softplus(-(_dot(z, wgk_ref[...]) + bgk_ref[...])) / GLA_GATE_NORM

    scale = GLA_DK ** -0.5
    causal = (lax.broadcasted_iota(jnp.int32, (CHUNK, LANES), 0)
              >= lax.broadcasted_iota(jnp.int32, (CHUNK, LANES), 1))
    zeros_c = jnp.zeros((CHUNK, LANES), F32)
    zeros_v = jnp.zeros((CHUNK, GLA_DV), BF16)
    heads = range(GLA_HEADS)
    kls = [slice(hd * GLA_DK, (hd + 1) * GLA_DK) for hd in heads]
    vls = [slice(hd * GLA_DV, (hd + 1) * GLA_DV) for hd in heads]

    def group_body(grp):
        prep = []
        for j in range(group):
            rows = _rows(grp * group + j, CHUNK)
            b = _cumsum_rows(g_ref[rows, :])
            b_last = b[CHUNK - 1:CHUNK, :]
            q = q_ref[rows, :]
            k = k_ref[rows, :]
            q_dec = q * jnp.exp(b) * scale
            k_inv = k * jnp.exp(-b)
            k_end = k * jnp.exp(b_last - b)
            decay = jnp.exp(b_last)
            vs = [v_ref[rows, vl] for vl in vls]
            scores = [_dot_nt(q_dec[:, kl], jnp.concatenate([k_inv[:, kl], zeros_c], axis=0))
                      for kl in kls]
            kvs = [jnp.dot(k_end[:, kls[hd]].T.astype(BF16), vs[hd], preferred_element_type=F32)
                   for hd in heads]
            decs = [jnp.broadcast_to(decay[:, kl], (GLA_DK, GLA_DK)).T for kl in kls]
            prep.append((rows, q_dec, vs, scores, kvs, decs))
        sts = [state_ref[hd] for hd in heads]
        for rows, q_dec, vs, scores, kvs, decs in prep:
            for hd in heads:
                a = jnp.where(causal, scores[hd], 0.0)
                qa = jnp.concatenate([q_dec[:, kls[hd]], a], axis=-1).astype(BF16)
                rhs = jnp.concatenate([sts[hd].astype(BF16), vs[hd], zeros_v], axis=0)
                att_ref[rows, vls[hd]] = jnp.dot(qa, rhs, preferred_element_type=F32)
            sts = [sts[hd] * jnp.concatenate([decs[hd], decs[hd]], axis=-1) + kvs[hd] for hd in heads]
        for hd in heads:
            state_ref[hd] = sts[hd]

    for grp in range(n_chunks // group):
        group_body(grp)

    o = att_ref[...]
    parts = []
    for hd in range(GLA_HEADS):
        oh = o[:, hd * GLA_DV:(hd + 1) * GLA_DV]
        parts.append(oh * lax.rsqrt(jnp.mean(oh * oh, axis=-1, keepdims=True) + EPS))
    on = jnp.concatenate(parts, axis=-1) * gw_ref[...]
    r = jnp.dot(h_ref[...], wr_ref[...], preferred_element_type=F32)
    y = on * (r * jax.nn.sigmoid(r))
    o_ref[...] = x_ref[...] + jnp.dot(y.astype(BF16), wo_ref[...], preferred_element_type=F32)


def _gla_layer(x, nw, w_in, w_gk, b_gk, gnorm_w, w_out, *, ts):
    bsz, s, d = x.shape
    key = GLA_HEADS * GLA_DK
    val = GLA_HEADS * GLA_DV
    rank = w_gk.shape[0]
    assert s % ts == 0 and ts % CHUNK == 0
    w = w_in.astype(BF16)
    wq, wk = w[:, :key], w[:, key:2 * key]
    wv, wr = w[:, 2 * key:2 * key + val], w[:, 2 * key + val:2 * key + 2 * val]
    wz = jnp.pad(w[:, 2 * key + 2 * val:], ((0, 0), (0, LANES - rank)))
    wgk = jnp.pad(w_gk.astype(BF16), ((0, LANES - rank), (0, 0)))
    n_chunks = ts // CHUNK
    group = 4
    assert n_chunks % group == 0
    kern = functools.partial(_gla_kernel, n_chunks=n_chunks, group=group)
    xspec = pl.BlockSpec((None, ts, d), lambda b, j: (b, j, 0))
    return pl.pallas_call(
        kern,
        grid=(bsz, s // ts),
        in_specs=[
            xspec,
            _resident((1, d)),
            _resident((d, key)), _resident((d, key)), _resident((d, val)), _resident((d, val)),
            _resident((d, LANES)), _resident((LANES, key)), _resident((1, key)),
            _resident((1, val)), _resident((val, d)),
        ],
        out_specs=xspec,
        out_shape=jax.ShapeDtypeStruct(x.shape, F32),
        scratch_shapes=[
            pltpu.VMEM((ts, d), BF16),
            pltpu.VMEM((ts, key), F32),
            pltpu.VMEM((ts, key), F32),
            pltpu.VMEM((ts, key), F32),
            pltpu.VMEM((ts, val), BF16),
            pltpu.VMEM((ts, val), F32),
            pltpu.VMEM((GLA_HEADS, GLA_DK, GLA_DV), F32),
        ],
        compiler_params=pltpu.CompilerParams(
            dimension_semantics=("arbitrary", "arbitrary"), vmem_limit_bytes=VMEM_LIMIT_BYTES),
        name="gla_layer",
    )(x, nw.reshape(1, d), wq, wk, wv, wr, wz, wgk, b_gk.reshape(1, key),
      jnp.tile(gnorm_w, GLA_HEADS).reshape(1, val), w_out.astype(BF16))


def _gdn_kernel(x_ref, nw_ref, wqkv_ref, wr_ref, wa_ref, wb_ref, cw_ref, alog_ref, dtb_ref,
                gw_ref, wo_ref, o_ref,
                h_ref, q_ref, k_ref, v_ref, g_ref, beta_ref, u_ref, wq_ref, kendt_ref, qk_ref,
                elast_ref, ext_ref, state_ref, *, n_chunks, group):
    ts = x_ref.shape[0]
    n_slabs = ext_ref.shape[0]

    @pl.when(pl.program_id(1) == 0)
    def _():
        state_ref[...] = jnp.zeros_like(state_ref)
        ext_ref[:, 0:SUBLANES, :] = jnp.zeros((n_slabs, SUBLANES, LANES), F32)

    @pl.when(pl.program_id(1) != 0)
    def _():
        ext_ref[:, 0:SUBLANES, :] = ext_ref[:, ts:ts + SUBLANES, :]

    h = _rmsnorm(x_ref[...], nw_ref[...]).astype(BF16)
    h_ref[...] = h

    pre = jnp.dot(h, wqkv_ref[...], preferred_element_type=F32)
    for s in range(n_slabs):
        ext_ref[s, SUBLANES:, :] = pre[:, s * LANES:(s + 1) * LANES]

    scale = GDN_DK ** -0.5
    rows_ph = ts // CONV_PHASES
    for s in range(n_slabs):
        cw = cw_ref[:, s * LANES:(s + 1) * LANES]
        first = SUBLANES - (CONV_K - 1)
        taps = {start: ext_ref[s, pl.ds(start, rows_ph, stride=CONV_PHASES), :]
                for start in range(first, SUBLANES + CONV_PHASES)}
        for r in range(CONV_PHASES):
            conv = taps[SUBLANES + r] * cw[CONV_K - 1:CONV_K, :]
            for back in range(1, CONV_K):
                conv = conv + taps[SUBLANES + r - back] * cw[CONV_K - 1 - back:CONV_K - back, :]
            y = conv * jax.nn.sigmoid(conv)
            dst = pl.ds(r, rows_ph, stride=CONV_PHASES)
            if s < GDN_HEADS:
                y = y * lax.rsqrt(jnp.sum(y * y, axis=-1, keepdims=True) + EPS) * scale
                q_ref[s, dst, :] = y
            elif s < 2 * GDN_HEADS:
                y = y * lax.rsqrt(jnp.sum(y * y, axis=-1, keepdims=True) + EPS)
                k_ref[s - GDN_HEADS, dst, :] = y
            else:
                v_ref[s - 2 * GDN_HEADS, dst, :] = y

    a = jnp.dot(h, wa_ref[...], preferred_element_type=F32)
    bl = jnp.dot(h, wb_ref[...], preferred_element_type=F32)
    beta_ref[...] = jax.nn.sigmoid(bl)
    g_ref[...] = -jnp.exp(alog_ref[...]) * _softplus(a + dtb_ref[...])

    row = lax.broadcasted_iota(jnp.int32, (CHUNK, LANES), 0)
    lane = lax.broadcasted_iota(jnp.int32, (CHUNK, LANES), 1)
    causal = row >= lane
    strict = row > lane
    upper = lane >= CHUNK
    eye_upper = jnp.where(lane == row + CHUNK, 1.0, 0.0)
    zeros_c = jnp.zeros((CHUNK, LANES), F32)
    zeros_rhs = jnp.zeros((CHUNK, GDN_DV + GDN_DK), F32)
    heads = range(GDN_HEADS)
    hls = [slice(hd * GDN_DK, (hd + 1) * GDN_DK) for hd in heads]

    def solve_stages(chunk_ids):
        chains = [(j, hd) for j in range(len(chunk_ids)) for hd in heads]
        t = {}

        def grams():
            rows = [_rows(c, CHUNK) for c in chunk_ids]
            rows2 = [_rows(c, 2 * CHUNK) for c in chunk_ids]
            dcum, dcum_t, beta, e_d, e_end = [], [], [], [], []
            for j, c in enumerate(chunk_ids):
                dcum.append(_cumsum_rows(g_ref[rows[j], :]))
                dcum_t.append(jnp.concatenate([dcum[j], zeros_c], axis=0).T)
                beta.append(beta_ref[rows[j], :])
                e_d.append(jnp.exp(dcum[j]))
                d_last = dcum[j][CHUNK - 1:CHUNK, :]
                e_end.append(jnp.exp(d_last - dcum[j]))
                elast_ref[_rows(c, SUBLANES), :] = jnp.broadcast_to(jnp.exp(d_last), (SUBLANES, LANES))
            ks = [k_ref[hd, rows[j], :] for j, hd in chains]
            qs = [q_ref[hd, rows[j], :] for j, hd in chains]
            kbs = [ks[i] * beta[j][:, hd:hd + 1] for i, (j, hd) in enumerate(chains)]
            t["grams"] = [_dot_nt(jnp.concatenate([kbs[i], qs[i]], axis=0),
                                  jnp.concatenate([ks[i], zeros_c], axis=0)) for i in range(len(chains))]
            t.update(rows=rows, rows2=rows2, dcum=dcum, dcum_t=dcum_t, beta=beta, e_d=e_d, e_end=e_end,
                     ks=ks, qs=qs, kbs=kbs)

        def double():
            prods = [_dot(w, jnp.concatenate([w, zeros_c], axis=0)) for w in t["ws"]]
            t["ws"] = [p + jnp.where(upper, w, 0.0) for p, w in zip(prods, t["ws"])]

        def mask_and_double():
            ws = []
            for i, (j, hd) in enumerate(chains):
                decay = jnp.exp(jnp.minimum(
                    t["dcum"][j][:, hd:hd + 1] - t["dcum_t"][j][hd:hd + 1, :], 0.0))
                gram = t["grams"][i]
                qk_ref[t["rows"][j], hls[hd]] = jnp.where(causal, gram[CHUNK:] * decay, 0.0).astype(BF16)
                ws.append(jnp.where(strict, -(gram[:CHUNK] * decay), 0.0) + eye_upper)
            t["ws"] = ws
            double()

        def apply():
            for i, (j, hd) in enumerate(chains):
                rows, rows2, e_d = t["rows"][j], t["rows2"][j], t["e_d"][j]
                rhs = jnp.concatenate([v_ref[hd, rows, :] * t["beta"][j][:, hd:hd + 1],
                                       t["kbs"][i] * e_d[:, hd:hd + 1]], axis=-1)
                sol = _dot(t["ws"][i], jnp.concatenate([zeros_rhs, rhs], axis=0))
                u_ref[rows, hls[hd]] = sol[:, :GDN_DV]
                wq_ref[rows2, hls[hd]] = jnp.concatenate(
                    [sol[:, GDN_DV:], t["qs"][i] * e_d[:, hd:hd + 1]], axis=0).astype(BF16)
                kend_t = (t["ks"][i] * t["e_end"][j][:, hd:hd + 1]).T
                kendt_ref[rows2, hd * LANES:hd * LANES + CHUNK] = kend_t.astype(BF16)

        n_double = CHUNK.bit_length() - 1
        return [grams, mask_and_double] + [double] * (n_double - 1) + [apply]

    def scan_stages(chunk_ids):
        t = {}

        def project(c, first):
            if first:
                t["sts"] = [state_ref[hd] for hd in heads]
            rows2 = _rows(c, 2 * CHUNK)
            t["wqs"] = [_dot(wq_ref[rows2, hls[hd]], t["sts"][hd]) for hd in heads]

        def update(c, last):
            rows, rows2 = _rows(c, CHUNK), _rows(c, 2 * CHUNK)
            e_last = elast_ref[_rows(c, SUBLANES), :][0:1, :]
            v_news = [u_ref[rows, hls[hd]] - t["wqs"][hd][:CHUNK] for hd in heads]
            for hd in heads:
                qk = qk_ref[rows, hd * LANES:hd * LANES + CHUNK]
                u_ref[rows, hls[hd]] = t["wqs"][hd][CHUNK:] + _dot(qk, v_news[hd])
            sts = []
            for hd in heads:
                kend_t = kendt_ref[rows2, hd * LANES:hd * LANES + CHUNK]
                sts.append(t["sts"][hd] * e_last[:, hd:hd + 1] + _dot(kend_t, v_news[hd]))
            t["sts"] = sts
            if last:
                for hd in heads:
                    state_ref[hd] = sts[hd]

        stages = []
        for idx, c in enumerate(chunk_ids):
            stages.append(functools.partial(project, c, idx == 0))
            stages.append(functools.partial(update, c, idx == len(chunk_ids) - 1))
        return stages

    n_groups = n_chunks // group
    group_ids = lambda g: [g * group + j for j in range(group)]
    _run(solve_stages(group_ids(0)))

    for g in range(n_groups - 1):
        _run(_interleave(solve_stages(group_ids(g + 1)), scan_stages(group_ids(g))))
    _run(scan_stages(group_ids(n_groups - 1)))
    att_ref = u_ref

    o = att_ref[...]
    parts = []
    for hd in range(GDN_HEADS):
        oh = o[:, hd * GDN_DV:(hd + 1) * GDN_DV]
        parts.append(oh * lax.rsqrt(jnp.mean(oh * oh, axis=-1, keepdims=True) + EPS))
    on = jnp.concatenate(parts, axis=-1) * gw_ref[...]
    r = jnp.dot(h_ref[...], wr_ref[...], preferred_element_type=F32)
    y = on * (r * jax.nn.sigmoid(r))
    o_ref[...] = x_ref[...] + jnp.dot(y.astype(BF16), wo_ref[...], preferred_element_type=F32)


def _gdn_layer(x, nw, w_in, conv_w, a_log, dt_bias, gnorm_w, w_out, *, ts):
    bsz, s, d = x.shape
    key = GDN_HEADS * GDN_DK
    val = GDN_HEADS * GDN_DV
    conv_ch = 2 * key + val
    assert s % ts == 0 and ts % CHUNK == 0
    w = w_in.astype(BF16)
    wqkv, wr = w[:, :conv_ch], w[:, conv_ch:conv_ch + val]
    pad = ((0, 0), (0, LANES - GDN_HEADS))
    wa = jnp.pad(w[:, conv_ch + val:conv_ch + val + GDN_HEADS], pad)
    wb = jnp.pad(w[:, conv_ch + val + GDN_HEADS:], pad)
    alog = jnp.pad(a_log.reshape(1, GDN_HEADS), pad)
    dtb = jnp.pad(dt_bias.reshape(1, GDN_HEADS), pad)
    n_chunks = ts // CHUNK
    group = 2
    assert n_chunks % group == 0
    kern = functools.partial(_gdn_kernel, n_chunks=n_chunks, group=group)
    xspec = pl.BlockSpec((None, ts, d), lambda b, j: (b, j, 0))
    return pl.pallas_call(
        kern,
        grid=(bsz, s // ts),
        in_specs=[
            xspec,
            _resident((1, d)),
            _resident((d, conv_ch)), _resident((d, val)), _resident((d, LANES)), _resident((d, LANES)),
            _resident((CONV_K, conv_ch)), _resident((1, LANES)), _resident((1, LANES)),
            _resident((1, val)), _resident((val, d)),
        ],
        out_specs=xspec,
        out_shape=jax.ShapeDtypeStruct(x.shape, F32),
        scratch_shapes=[
            pltpu.VMEM((ts, d), BF16),
            pltpu.VMEM((GDN_HEADS, ts, GDN_DK), F32),
            pltpu.VMEM((GDN_HEADS, ts, GDN_DK), F32),
            pltpu.VMEM((GDN_HEADS, ts, GDN_DV), F32),
            pltpu.VMEM((ts, LANES), F32),
            pltpu.VMEM((ts, LANES), F32),
            pltpu.VMEM((ts, val), F32),
            pltpu.VMEM((2 * ts, key), BF16),
            pltpu.VMEM((2 * ts, key), BF16),
            pltpu.VMEM((ts, key), BF16),
            pltpu.VMEM((ts // CHUNK * SUBLANES, LANES), F32),
            pltpu.VMEM((conv_ch // LANES, ts + SUBLANES, LANES), F32),
            pltpu.VMEM((GDN_HEADS, GDN_DK, GDN_DV), F32),
        ],
        compiler_params=pltpu.CompilerParams(
            dimension_semantics=("arbitrary", "arbitrary"), vmem_limit_bytes=VMEM_LIMIT_BYTES),
        name="gdn_layer",
    )(x, nw.reshape(1, d), wqkv, wr, wa, wb, conv_w, alog, dtb,
      jnp.tile(gnorm_w, GDN_HEADS).reshape(1, val), w_out.astype(BF16))


def _trunk(x, norm_w, ffn_w_in, ffn_w_out, gla_w_in, gla_w_gk, gla_b_gk, gla_norm_w, gla_w_out,
           gdn_w_in, gdn_conv_w, gdn_a_log, gdn_dt_bias, gdn_norm_w, gdn_w_out, final_norm_w):
    bsz, s, d = x.shape
    depth = norm_w.shape[0]
    for i in range(depth):
        x = _ffn(x.reshape(bsz * s, d), norm_w[i, 0], ffn_w_in[i, 0], ffn_w_out[i, 0],
                 final_norm_w, tm=FFN_ROWS, final_norm=False).reshape(bsz, s, d)
        j = i // 2
        if i % 2 == 0:
            x = _gla_layer(x, norm_w[i, 1], gla_w_in[j], gla_w_gk[j], gla_b_gk[j], gla_norm_w[j],
                           gla_w_out[j], ts=GLA_ROWS)
        else:
            x = _gdn_layer(x, norm_w[i, 1], gdn_w_in[j], gdn_conv_w[j], gdn_a_log[j],
                           gdn_dt_bias[j], gdn_norm_w[j], gdn_w_out[j], ts=GDN_ROWS)
        x = _ffn(x.reshape(bsz * s, d), norm_w[i, 2], ffn_w_in[i, 1], ffn_w_out[i, 1],
                 final_norm_w, tm=FFN_ROWS, final_norm=(i == depth - 1)).reshape(bsz, s, d)
    return x


def kernel(x, norm_w, ffn_w_in, ffn_w_out, gla_w_in, gla_w_gk, gla_b_gk, gla_norm_w, gla_w_out,
           gdn_w_in, gdn_conv_w, gdn_a_log, gdn_dt_bias, gdn_norm_w, gdn_w_out, final_norm_w):
    return _trunk(x, norm_w, ffn_w_in, ffn_w_out, gla_w_in, gla_w_gk, gla_b_gk, gla_norm_w,
                  gla_w_out, gdn_w_in, gdn_conv_w, gdn_a_log, gdn_dt_bias, gdn_norm_w, gdn_w_out,
                  final_norm_w)
```

```python
import functools

import jax
import jax.numpy as jnp
from jax import lax
from jax.experimental import pallas as pl
from jax.experimental.pallas import tpu as pltpu

F32 = jnp.float32
BF16 = jnp.bfloat16

EPS = 1e-6
CHUNK = 64
LANES = 128
SUBLANES = 8
MXU_N = 256

GLA_HEADS, GLA_DK, GLA_DV = 4, 128, 256
GLA_GATE_NORM = 16.0
GDN_HEADS, GDN_DK, GDN_DV = 8, 128, 128
CONV_K = 4
CONV_PHASES = 4

VMEM_LIMIT_BYTES = 60000 * 1024

FFN_ROWS = 1024
GLA_ROWS = 1024
GDN_ROWS = 512


def _rmsnorm(x, w):
    return x * lax.rsqrt(jnp.mean(x * x, axis=-1, keepdims=True) + EPS) * w


def _dot(a, b):
    return jnp.dot(a.astype(BF16), b.astype(BF16), preferred_element_type=F32)


def _dot_nt(a, b):
    return lax.dot_general(a.astype(BF16), b.astype(BF16), (((1,), (1,)), ((), ())),
                           preferred_element_type=F32)


def _dot_tn(a, b):
    return jnp.dot(a.astype(F32).T.astype(BF16), b.astype(BF16), preferred_element_type=F32)


def _softplus(x):
    return jnp.maximum(x, 0.0) + jnp.log1p(jnp.exp(-jnp.abs(x)))


def _rows(index, size):
    start = index * size
    return pl.ds(start if isinstance(start, int) else pl.multiple_of(start, size), size)


def _run(stages):
    for stage in stages:
        stage()


def _interleave(major, minor):
    out = []
    for i, stage in enumerate(major):
        out.append(stage)
        out.extend(m for k, m in enumerate(minor) if (k * len(major)) // len(minor) == i)
    return out


def _cumsum_rows(x):
    rows = x.shape[0]
    row = lax.broadcasted_iota(jnp.int32, x.shape, 0)
    shift = 1
    while shift < rows:
        x = x + jnp.where(row >= shift, pltpu.roll(x, shift, axis=0), 0.0)
        shift *= 2
    return x


def _ffn_kernel(x_ref, nw_ref, wi_ref, wo_ref, fw_ref, o_ref, h_ref, acc_ref, *, final_norm):
    f = wo_ref.shape[0]
    h_ref[...] = _rmsnorm(x_ref[...], nw_ref[...]).astype(BF16)
    acc_ref[...] = jnp.zeros_like(acc_ref)
    for lo in range(0, f, MXU_N):
        h = h_ref[...]
        g = jnp.dot(h, wi_ref[:, lo:lo + MXU_N], preferred_element_type=F32)
        u = jnp.dot(h, wi_ref[:, f + lo:f + lo + MXU_N], preferred_element_type=F32)
        a = (g * jax.nn.sigmoid(g) * u).astype(BF16)
        acc_ref[...] += jnp.dot(a, wo_ref[lo:lo + MXU_N, :], preferred_element_type=F32)
    y = x_ref[...] + 0.5 * acc_ref[...]
    if final_norm:
        y = _rmsnorm(y, fw_ref[...])
    o_ref[...] = y


def _resident(shape):
    nd = len(shape)
    return pl.BlockSpec(shape, lambda *_: (0,) * nd, pipeline_mode=pl.Buffered(1))


def _ffn(x2d, nw, w_in_all, w_out_all, layer, which, final_w, *, tm, final_norm):
    m, d = x2d.shape
    f = w_out_all.shape[2]
    assert f % MXU_N == 0 and m % tm == 0
    kern = functools.partial(_ffn_kernel, final_norm=final_norm)
    pick = lambda i: (layer, which, 0, 0)
    return pl.pallas_call(
        kern,
        grid=(m // tm,),
        in_specs=[
            pl.BlockSpec((tm, d), lambda i: (i, 0)),
            _resident((1, d)),
            pl.BlockSpec((None, None, d, 2 * f), pick, pipeline_mode=pl.Buffered(1)),
            pl.BlockSpec((None, None, f, d), pick, pipeline_mode=pl.Buffered(1)),
            _resident((1, d)),
        ],
        out_specs=pl.BlockSpec((tm, d), lambda i: (i, 0)),
        out_shape=jax.ShapeDtypeStruct((m, d), F32),
        scratch_shapes=[pltpu.VMEM((tm, d), BF16), pltpu.VMEM((tm, d), F32)],
        compiler_params=pltpu.CompilerParams(
            dimension_semantics=("arbitrary",), vmem_limit_bytes=VMEM_LIMIT_BYTES),
        name="ffn",
    )(x2d, nw.reshape(1, d), w_in_all, w_out_all, final_w.reshape(1, d))


def _gla_kernel(x_ref, nw_ref, wq_ref, wk_ref, wv_ref, wr_ref, wz_ref, wgk_ref, bgk_ref,
                gw_ref, wo_ref, o_ref,
                h_ref, q_ref, k_ref, g_ref, v_ref, att_ref, state_ref, *, n_chunks, group):
    @pl.when(pl.program_id(1) == 0)
    def _():
        state_ref[...] = jnp.zeros_like(state_ref)

    h = _rmsnorm(x_ref[...], nw_ref[...]).astype(BF16)
    h_ref[...] = h
    q_ref[...] = jnp.dot(h, wq_ref[...], preferred_element_type=F32)
    k_ref[...] = jnp.dot(h, wk_ref[...], preferred_element_type=F32)
    v_ref[...] = jnp.dot(h, wv_ref[...], preferred_element_type=F32).astype(BF16)
    z = jnp.dot(h, wz_ref[...], preferred_element_type=F32)
    g_ref[...] = -_softplus(-(_dot(z, wgk_ref[...]) + bgk_ref[...])) / GLA_GATE_NORM

    scale = GLA_DK ** -0.5
    causal = (lax.broadcasted_iota(jnp.int32, (CHUNK, LANES), 0)
              >= lax.broadcasted_iota(jnp.int32, (CHUNK, LANES), 1))
    zeros_c = jnp.zeros((CHUNK, LANES), F32)
    zeros_v = jnp.zeros((CHUNK, GLA_DV), BF16)
    heads = range(GLA_HEADS)
    kls = [slice(hd * GLA_DK, (hd + 1) * GLA_DK) for hd in heads]
    vls = [slice(hd * GLA_DV, (hd + 1) * GLA_DV) for hd in heads]

    def group_body(grp):
        prep = []
        for j in range(group):
            rows = _rows(grp * group + j, CHUNK)
            b = _cumsum_rows(g_ref[rows, :])
            b_last = b[CHUNK - 1:CHUNK, :]
            q = q_ref[rows, :]
            k = k_ref[rows, :]
            q_dec = q * jnp.exp(b) * scale
            k_inv = k * jnp.exp(-b)
            k_end = k * jnp.exp(b_last - b)
            decay = jnp.exp(b_last)
            vs = [v_ref[rows, vl] for vl in vls]
            scores = [_dot_nt(q_dec[:, kl], jnp.concatenate([k_inv[:, kl], zeros_c], axis=0))
                      for kl in kls]
            kvs = [jnp.dot(k_end[:, kls[hd]].T.astype(BF16), vs[hd], preferred_element_type=F32)
                   for hd in heads]
            decs = [jnp.broadcast_to(decay[:, kl], (GLA_DK, GLA_DK)).T for kl in kls]
            prep.append((rows, q_dec, vs, scores, kvs, decs))
        sts = [state_ref[hd] for hd in heads]
        for rows, q_dec, vs, scores, kvs, decs in prep:
            for hd in heads:
                a = jnp.where(causal, scores[hd], 0.0)
                qa = jnp.concatenate([q_dec[:, kls[hd]], a], axis=-1).astype(BF16)
                rhs = jnp.concatenate([sts[hd].astype(BF16), vs[hd], zeros_v], axis=0)
                att_ref[rows, vls[hd]] = jnp.dot(qa, rhs, preferred_element_type=F32)
            sts = [sts[hd] * jnp.concatenate([decs[hd], decs[hd]], axis=-1) + kvs[hd] for hd in heads]
        for hd in heads:
            state_ref[hd] = sts[hd]

    for grp in range(n_chunks // group):
        group_body(grp)

    o = att_ref[...]
    parts = []
    for hd in range(GLA_HEADS):
        oh = o[:, hd * GLA_DV:(hd + 1) * GLA_DV]
        parts.append(oh * lax.rsqrt(jnp.mean(oh * oh, axis=-1, keepdims=True) + EPS))
    on = jnp.concatenate(parts, axis=-1) * gw_ref[...]
    r = jnp.dot(h_ref[...], wr_ref[...], preferred_element_type=F32)
    y = on * (r * jax.nn.sigmoid(r))
    o_ref[...] = x_ref[...] + jnp.dot(y.astype(BF16), wo_ref[...], preferred_element_type=F32)


def _gla_layer(x, nw, w_in, w_gk, b_gk, gnorm_w, w_out, *, ts):
    bsz, s, d = x.shape
    key = GLA_HEADS * GLA_DK
    val = GLA_HEADS * GLA_DV
    rank = w_gk.shape[0]
    assert s % ts == 0 and ts % CHUNK == 0
    w = w_in.astype(BF16)
    wq, wk = w[:, :key], w[:, key:2 * key]
    wv, wr = w[:, 2 * key:2 * key + val], w[:, 2 * key + val:2 * key + 2 * val]
    wz = jnp.pad(w[:, 2 * key + 2 * val:], ((0, 0), (0, LANES - rank)))
    wgk = jnp.pad(w_gk.astype(BF16), ((0, LANES - rank), (0, 0)))
    n_chunks = ts // CHUNK
    group = 4
    assert n_chunks % group == 0
    kern = functools.partial(_gla_kernel, n_chunks=n_chunks, group=group)
    xspec = pl.BlockSpec((None, ts, d), lambda b, j: (b, j, 0))
    return pl.pallas_call(
        kern,
        grid=(bsz, s // ts),
        in_specs=[
            xspec,
            _resident((1, d)),
            _resident((d, key)), _resident((d, key)), _resident((d, val)), _resident((d, val)),
            _resident((d, LANES)), _resident((LANES, key)), _resident((1, key)),
            _resident((1, val)), _resident((val, d)),
        ],
        out_specs=xspec,
        out_shape=jax.ShapeDtypeStruct(x.shape, F32),
        scratch_shapes=[
            pltpu.VMEM((ts, d), BF16),
            pltpu.VMEM((ts, key), F32),
            pltpu.VMEM((ts, key), F32),
            pltpu.VMEM((ts, key), F32),
            pltpu.VMEM((ts, val), BF16),
            pltpu.VMEM((ts, val), F32),
            pltpu.VMEM((GLA_HEADS, GLA_DK, GLA_DV), F32),
        ],
        compiler_params=pltpu.CompilerParams(
            dimension_semantics=("arbitrary", "arbitrary"), vmem_limit_bytes=VMEM_LIMIT_BYTES),
        name="gla_layer",
    )(x, nw.reshape(1, d), wq, wk, wv, wr, wz, wgk, b_gk.reshape(1, key),
      jnp.tile(gnorm_w, GLA_HEADS).reshape(1, val), w_out.astype(BF16))


def _gdn_kernel(x_ref, nw_ref, wqkv_ref, wr_ref, wab_ref, cw_ref, alog_ref, dtb_ref,
                gw_ref, wo_ref, o_ref,
                h_ref, q_ref, k_ref, v_ref, g_ref, beta_ref, u_ref, wq_ref, kendt_ref, qk_ref,
                elast_ref, ext_ref, state_ref, *, n_chunks, group):
    ts = x_ref.shape[0]
    n_slabs = ext_ref.shape[0]

    @pl.when(pl.program_id(1) == 0)
    def _():
        state_ref[...] = jnp.zeros_like(state_ref)
        ext_ref[:, 0:SUBLANES, :] = jnp.zeros((n_slabs, SUBLANES, LANES), F32)

    @pl.when(pl.program_id(1) != 0)
    def _():
        ext_ref[:, 0:SUBLANES, :] = ext_ref[:, ts:ts + SUBLANES, :]

    h = _rmsnorm(x_ref[...], nw_ref[...]).astype(BF16)
    h_ref[...] = h

    pre = jnp.dot(h, wqkv_ref[...], preferred_element_type=F32)
    for s in range(n_slabs):
        ext_ref[s, SUBLANES:, :] = pre[:, s * LANES:(s + 1) * LANES]

    scale = GDN_DK ** -0.5
    rows_ph = ts // CONV_PHASES
    for s in range(n_slabs):
        cw = cw_ref[:, s * LANES:(s + 1) * LANES]
        first = SUBLANES - (CONV_K - 1)
        taps = {start: ext_ref[s, pl.ds(start, rows_ph, stride=CONV_PHASES), :]
                for start in range(first, SUBLANES + CONV_PHASES)}
        for r in range(CONV_PHASES):
            conv = taps[SUBLANES + r] * cw[CONV_K - 1:CONV_K, :]
            for back in range(1, CONV_K):
                conv = conv + taps[SUBLANES + r - back] * cw[CONV_K - 1 - back:CONV_K - back, :]
            y = conv * jax.nn.sigmoid(conv)
            dst = pl.ds(r, rows_ph, stride=CONV_PHASES)
            if s < GDN_HEADS:
                y = y * lax.rsqrt(jnp.sum(y * y, axis=-1, keepdims=True) + EPS) * scale
                q_ref[s, dst, :] = y
            elif s < 2 * GDN_HEADS:
                y = y * lax.rsqrt(jnp.sum(y * y, axis=-1, keepdims=True) + EPS)
                k_ref[s - GDN_HEADS, dst, :] = y
            else:
                v_ref[s - 2 * GDN_HEADS, dst, :] = y

    a = jnp.dot(h, wab_ref[...], preferred_element_type=F32)
    bl = pltpu.roll(a, LANES - GDN_HEADS, axis=1)
    beta_ref[...] = jax.nn.sigmoid(bl)
    g_ref[...] = -jnp.exp(alog_ref[...]) * _softplus(a + dtb_ref[...])

    row = lax.broadcasted_iota(jnp.int32, (CHUNK, LANES), 0)
    lane = lax.broadcasted_iota(jnp.int32, (CHUNK, LANES), 1)
    causal = row >= lane
    strict = row > lane
    upper = lane >= CHUNK
    eye_upper = jnp.where(lane == row + CHUNK, 1.0, 0.0)
    zeros_c = jnp.zeros((CHUNK, LANES), F32)
    zeros_rhs = jnp.zeros((CHUNK, GDN_DV + GDN_DK), F32)
    heads = range(GDN_HEADS)
    hls = [slice(hd * GDN_DK, (hd + 1) * GDN_DK) for hd in heads]

    def solve_stages(chunk_ids):
        chains = [(j, hd) for j in range(len(chunk_ids)) for hd in heads]
        t = {}

        def grams():
            rows = [_rows(c, CHUNK) for c in chunk_ids]
            rows2 = [_rows(c, 2 * CHUNK) for c in chunk_ids]
            dcum, dcum_t, beta, e_d, e_end = [], [], [], [], []
            for j, c in enumerate(chunk_ids):
                dcum.append(_cumsum_rows(g_ref[rows[j], :]))
                dcum_t.append(jnp.concatenate([dcum[j], zeros_c], axis=0).T)
                beta.append(beta_ref[rows[j], :])
                e_d.append(jnp.exp(dcum[j]))
                d_last = dcum[j][CHUNK - 1:CHUNK, :]
                e_end.append(jnp.exp(d_last - dcum[j]))
                elast_ref[_rows(c, SUBLANES), :] = jnp.broadcast_to(jnp.exp(d_last), (SUBLANES, LANES))
            ks = [k_ref[hd, rows[j], :] for j, hd in chains]
            qs = [q_ref[hd, rows[j], :] for j, hd in chains]
            kbs = [ks[i] * beta[j][:, hd:hd + 1] for i, (j, hd) in enumerate(chains)]
            t["grams"] = [_dot_nt(jnp.concatenate([kbs[i], qs[i]], axis=0),
                                  jnp.concatenate([ks[i], zeros_c], axis=0)) for i in range(len(chains))]
            t.update(rows=rows, rows2=rows2, dcum=dcum, dcum_t=dcum_t, beta=beta, e_d=e_d, e_end=e_end,
                     ks=ks, qs=qs, kbs=kbs)

        def double():
            prods = [_dot(w, jnp.concatenate([w, zeros_c], axis=0)) for w in t["ws"]]
            t["ws"] = [p + jnp.where(upper, w, 0.0) for p, w in zip(prods, t["ws"])]

        def mask_and_double():
            ws = []
            for i, (j, hd) in enumerate(chains):
                decay = jnp.exp(jnp.minimum(
                    t["dcum"][j][:, hd:hd + 1] - t["dcum_t"][j][hd:hd + 1, :], 0.0))
                gram = t["grams"][i]
                qk_ref[t["rows"][j], hls[hd]] = jnp.where(causal, gram[CHUNK:] * decay, 0.0).astype(BF16)
                ws.append(jnp.where(strict, -(gram[:CHUNK] * decay), 0.0) + eye_upper)
            t["ws"] = ws
            double()

        def apply():
            for i, (j, hd) in enumerate(chains):
                rows, rows2, e_d = t["rows"][j], t["rows2"][j], t["e_d"][j]
                rhs = jnp.concatenate([v_ref[hd, rows, :] * t["beta"][j][:, hd:hd + 1],
                                       t["kbs"][i] * e_d[:, hd:hd + 1]], axis=-1)
                sol = _dot(t["ws"][i], jnp.concatenate([zeros_rhs, rhs], axis=0))
                u_ref[rows, hls[hd]] = sol[:, :GDN_DV]
                wq_ref[rows2, hls[hd]] = jnp.concatenate(
                    [sol[:, GDN_DV:], t["qs"][i] * e_d[:, hd:hd + 1]], axis=0).astype(BF16)
                kend_t = (t["ks"][i] * t["e_end"][j][:, hd:hd + 1]).T
                kendt_ref[rows2, hd * LANES:hd * LANES + CHUNK] = kend_t.astype(BF16)

        n_double = CHUNK.bit_length() - 1
        return [grams, mask_and_double] + [double] * (n_double - 1) + [apply]

    def scan_stages(chunk_ids):
        t = {}

        def project(c, first):
            if first:
                t["sts"] = [state_ref[hd] for hd in heads]
            rows2 = _rows(c, 2 * CHUNK)
            t["wqs"] = [_dot(wq_ref[rows2, hls[hd]], t["sts"][hd]) for hd in heads]

        def update(c, last):
            rows, rows2 = _rows(c, CHUNK), _rows(c, 2 * CHUNK)
            e_last = elast_ref[_rows(c, SUBLANES), :][0:1, :]
            v_news = [u_ref[rows, hls[hd]] - t["wqs"][hd][:CHUNK] for hd in heads]
            for hd in heads:
                qk = qk_ref[rows, hd * LANES:hd * LANES + CHUNK]
                u_ref[rows, hls[hd]] = t["wqs"][hd][CHUNK:] + _dot(qk, v_news[hd])
            sts = []
            for hd in heads:
                kend_t = kendt_ref[rows2, hd * LANES:hd * LANES + CHUNK]
                sts.append(t["sts"][hd] * e_last[:, hd:hd + 1] + _dot(kend_t, v_news[hd]))
            t["sts"] = sts
            if last:
                for hd in heads:
                    state_ref[hd] = sts[hd]

        stages = []
        for idx, c in enumerate(chunk_ids):
            stages.append(functools.partial(project, c, idx == 0))
            stages.append(functools.partial(update, c, idx == len(chunk_ids) - 1))
        return stages

    n_groups = n_chunks // group
    group_ids = lambda g: [g * group + j for j in range(group)]
    _run(solve_stages(group_ids(0)))

    for g in range(n_groups - 1):
        _run(_interleave(solve_stages(group_ids(g + 1)), scan_stages(group_ids(g))))
    _run(scan_stages(group_ids(n_groups - 1)))
    att_ref = u_ref

    o = att_ref[...]
    parts = []
    for hd in range(GDN_HEADS):
        oh = o[:, hd * GDN_DV:(hd + 1) * GDN_DV]
        parts.append(oh * lax.rsqrt(jnp.mean(oh * oh, axis=-1, keepdims=True) + EPS))
    on = jnp.concatenate(parts, axis=-1) * gw_ref[...]
    r = jnp.dot(h_ref[...], wr_ref[...], preferred_element_type=F32)
    y = on * (r * jax.nn.sigmoid(r))
    o_ref[...] = x_ref[...] + jnp.dot(y.astype(BF16), wo_ref[...], preferred_element_type=F32)


def _gdn_layer(x, nw, w_in, conv_w, a_log, dt_bias, gnorm_w, w_out, *, ts):
    bsz, s, d = x.shape
    key = GDN_HEADS * GDN_DK
    val = GDN_HEADS * GDN_DV
    conv_ch = 2 * key + val
    assert s % ts == 0 and ts % CHUNK == 0
    w = w_in.astype(BF16)
    wqkv, wr = w[:, :conv_ch], w[:, conv_ch:conv_ch + val]
    pad = ((0, 0), (0, LANES - GDN_HEADS))
    wab = jnp.pad(w[:, conv_ch + val:], ((0, 0), (0, LANES - 2 * GDN_HEADS)))
    alog = jnp.pad(a_log.reshape(1, GDN_HEADS), pad)
    dtb = jnp.pad(dt_bias.reshape(1, GDN_HEADS), pad)
    n_chunks = ts // CHUNK
    group = 2
    assert n_chunks % group == 0
    kern = functools.partial(_gdn_kernel, n_chunks=n_chunks, group=group)
    xspec = pl.BlockSpec((None, ts, d), lambda b, j: (b, j, 0))
    return pl.pallas_call(
        kern,
        grid=(bsz, s // ts),
        in_specs=[
            xspec,
            _resident((1, d)),
            _resident((d, conv_ch)), _resident((d, val)), _resident((d, LANES)),
            _resident((CONV_K, conv_ch)), _resident((1, LANES)), _resident((1, LANES)),
            _resident((1, val)), _resident((val, d)),
        ],
        out_specs=xspec,
        out_shape=jax.ShapeDtypeStruct(x.shape, F32),
        scratch_shapes=[
            pltpu.VMEM((ts, d), BF16),
            pltpu.VMEM((GDN_HEADS, ts, GDN_DK), F32),
            pltpu.VMEM((GDN_HEADS, ts, GDN_DK), F32),
            pltpu.VMEM((GDN_HEADS, ts, GDN_DV), F32),
            pltpu.VMEM((ts, LANES), F32),
            pltpu.VMEM((ts, LANES), F32),
            pltpu.VMEM((ts, val), F32),
            pltpu.VMEM((2 * ts, key), BF16),
            pltpu.VMEM((2 * ts, key), BF16),
            pltpu.VMEM((ts, key), BF16),
            pltpu.VMEM((ts // CHUNK * SUBLANES, LANES), F32),
            pltpu.VMEM((conv_ch // LANES, ts + SUBLANES, LANES), F32),
            pltpu.VMEM((GDN_HEADS, GDN_DK, GDN_DV), F32),
        ],
        compiler_params=pltpu.CompilerParams(
            dimension_semantics=("arbitrary", "arbitrary"), vmem_limit_bytes=VMEM_LIMIT_BYTES),
        name="gdn_layer",
    )(x, nw.reshape(1, d), wqkv, wr, wab, conv_w, alog, dtb,
      jnp.tile(gnorm_w, GDN_HEADS).reshape(1, val), w_out.astype(BF16))


def _trunk(x, norm_w, ffn_w_in, ffn_w_out, gla_w_in, gla_w_gk, gla_b_gk, gla_norm_w, gla_w_out,
           gdn_w_in, gdn_conv_w, gdn_a_log, gdn_dt_bias, gdn_norm_w, gdn_w_out, final_norm_w):
    bsz, s, d = x.shape
    depth = norm_w.shape[0]
    ffn_w_in = ffn_w_in.astype(BF16)
    ffn_w_out = ffn_w_out.astype(BF16)
    for i in range(depth):
        x = _ffn(x.reshape(bsz * s, d), norm_w[i, 0], ffn_w_in, ffn_w_out, i, 0,
                 final_norm_w, tm=FFN_ROWS, final_norm=False).reshape(bsz, s, d)
        j = i // 2
        if i % 2 == 0:
            x = _gla_layer(x, norm_w[i, 1], gla_w_in[j], gla_w_gk[j], gla_b_gk[j], gla_norm_w[j],
                           gla_w_out[j], ts=GLA_ROWS)
        else:
            x = _gdn_layer(x, norm_w[i, 1], gdn_w_in[j], gdn_conv_w[j], gdn_a_log[j],
                           gdn_dt_bias[j], gdn_norm_w[j], gdn_w_out[j], ts=GDN_ROWS)
        x = _ffn(x.reshape(bsz * s, d), norm_w[i, 2], ffn_w_in, ffn_w_out, i, 1,
                 final_norm_w, tm=FFN_ROWS, final_norm=(i == depth - 1)).reshape(bsz, s, d)
    return x


def kernel(x, norm_w, ffn_w_in, ffn_w_out, gla_w_in, gla_w_gk, gla_b_gk, gla_norm_w, gla_w_out,
           gdn_w_in, gdn_conv_w, gdn_a_log, gdn_dt_bias, gdn_norm_w, gdn_w_out, final_norm_w):
    return _trunk(x, norm_w, ffn_w_in, ffn_w_out, gla_w_in, gla_w_gk, gla_b_gk, gla_norm_w,
                  gla_w_out, gdn_w_in, gdn_conv_w, gdn_a_log, gdn_dt_bias, gdn_norm_w, gdn_w_out,
                  final_norm_w)
```

```python
import functools

import jax
import jax.numpy as jnp
from jax import lax
from jax.experimental import pallas as pl
from jax.experimental.pallas import tpu as pltpu

F32 = jnp.float32
BF16 = jnp.bfloat16

EPS = 1e-6
CHUNK = 64
LANES = 128
SUBLANES = 8
MXU_N = 256

GLA_HEADS, GLA_DK, GLA_DV = 4, 128, 256
GLA_GATE_NORM = 16.0
GDN_HEADS, GDN_DK, GDN_DV = 8, 128, 128
CONV_K = 4
CONV_PHASES = 4

VMEM_LIMIT_BYTES = 60000 * 1024

FFN_ROWS = 1024
GLA_ROWS = 1024
GDN_ROWS = 512


def _rmsnorm(x, w):
    return x * lax.rsqrt(jnp.mean(x * x, axis=-1, keepdims=True) + EPS) * w


def _dot(a, b):
    return jnp.dot(a.astype(BF16), b.astype(BF16), preferred_element_type=F32)


def _dot_nt(a, b):
    return lax.dot_general(a.astype(BF16), b.astype(BF16), (((1,), (1,)), ((), ())),
                           preferred_element_type=F32)


def _dot_tn(a, b):
    return jnp.dot(a.astype(F32).T.astype(BF16), b.astype(BF16), preferred_element_type=F32)


def _softplus(x):
    return jnp.maximum(x, 0.0) + jnp.log1p(jnp.exp(-jnp.abs(x)))


def _rows(index, size):
    start = index * size
    return pl.ds(start if isinstance(start, int) else pl.multiple_of(start, size), size)


def _run(stages):
    for stage in stages:
        stage()


def _interleave(major, minor):
    out = []
    for i, stage in enumerate(major):
        out.append(stage)
        out.extend(m for k, m in enumerate(minor) if (k * len(major)) // len(minor) == i)
    return out


def _cumsum_rows(x):
    rows = x.shape[0]
    row = lax.broadcasted_iota(jnp.int32, x.shape, 0)
    shift = 1
    while shift < rows:
        x = x + jnp.where(row >= shift, pltpu.roll(x, shift, axis=0), 0.0)
        shift *= 2
    return x


def _ffn_kernel(x_ref, nw_ref, wi_ref, wo_ref, fw_ref, o_ref, h_ref, acc_ref, *, final_norm):
    f = wo_ref.shape[0]
    h_ref[...] = _rmsnorm(x_ref[...], nw_ref[...]).astype(BF16)
    acc_ref[...] = jnp.zeros_like(acc_ref)
    for lo in range(0, f, MXU_N):
        h = h_ref[...]
        g = jnp.dot(h, wi_ref[:, lo:lo + MXU_N], preferred_element_type=F32)
        u = jnp.dot(h, wi_ref[:, f + lo:f + lo + MXU_N], preferred_element_type=F32)
        a = (g * jax.nn.sigmoid(g) * u).astype(BF16)
        acc_ref[...] += jnp.dot(a, wo_ref[lo:lo + MXU_N, :], preferred_element_type=F32)
    y = x_ref[...] + 0.5 * acc_ref[...]
    if final_norm:
        y = _rmsnorm(y, fw_ref[...])
    o_ref[...] = y


def _resident(shape):
    nd = len(shape)
    return pl.BlockSpec(shape, lambda *_: (0,) * nd, pipeline_mode=pl.Buffered(1))


def _ffn(x2d, nw, w_in_all, w_out_all, layer, which, final_w, *, tm, final_norm):
    m, d = x2d.shape
    f = w_out_all.shape[2]
    assert f % MXU_N == 0 and m % tm == 0
    kern = functools.partial(_ffn_kernel, final_norm=final_norm)
    pick = lambda i: (layer, which, 0, 0)
    return pl.pallas_call(
        kern,
        grid=(m // tm,),
        in_specs=[
            pl.BlockSpec((tm, d), lambda i: (i, 0)),
            _resident((1, d)),
            pl.BlockSpec((None, None, d, 2 * f), pick, pipeline_mode=pl.Buffered(1)),
            pl.BlockSpec((None, None, f, d), pick, pipeline_mode=pl.Buffered(1)),
            _resident((1, d)),
        ],
        out_specs=pl.BlockSpec((tm, d), lambda i: (i, 0)),
        out_shape=jax.ShapeDtypeStruct((m, d), F32),
        scratch_shapes=[pltpu.VMEM((tm, d), BF16), pltpu.VMEM((tm, d), F32)],
        compiler_params=pltpu.CompilerParams(
            dimension_semantics=("arbitrary",), vmem_limit_bytes=VMEM_LIMIT_BYTES),
        name="ffn",
    )(x2d, nw.reshape(1, d), w_in_all, w_out_all, final_w.reshape(1, d))


def _gla_kernel(x_ref, nw_ref, wq_ref, wk_ref, wv_ref, wr_ref, wz_ref, wgk_ref, bgk_ref,
                gw_ref, wo_ref, o_ref,
                h_ref, q_ref, k_ref, g_ref, v_ref, att_ref, state_ref, *, n_chunks, group):
    @pl.when(pl.program_id(1) == 0)
    def _():
        state_ref[...] = jnp.zeros_like(state_ref)

    h = _rmsnorm(x_ref[...], nw_ref[...]).astype(BF16)
    h_ref[...] = h
    q_ref[...] = jnp.dot(h, wq_ref[...], preferred_element_type=F32)
    k_ref[...] = jnp.dot(h, wk_ref[...], preferred_element_type=F32)
    v_ref[...] = jnp.dot(h, wv_ref[...], preferred_element_type=F32).astype(BF16)
    z = jnp.dot(h, wz_ref[...], preferred_element_type=F32)
    g_ref[...] = -_softplus(-(_dot(z, wgk_ref[...]) + bgk_ref[...])) / GLA_GATE_NORM

    scale = GLA_DK ** -0.5
    causal = (lax.broadcasted_iota(jnp.int32, (CHUNK, LANES), 0)
              >= lax.broadcasted_iota(jnp.int32, (CHUNK, LANES), 1))
    zeros_c = jnp.zeros((CHUNK, LANES), F32)
    zeros_v = jnp.zeros((CHUNK, GLA_DV), BF16)
    heads = range(GLA_HEADS)
    kls = [slice(hd * GLA_DK, (hd + 1) * GLA_DK) for hd in heads]
    vls = [slice(hd * GLA_DV, (hd + 1) * GLA_DV) for hd in heads]

    def group_body(grp):
        prep = []
        for j in range(group):
            rows = _rows(grp * group + j, CHUNK)
            b = _cumsum_rows(g_ref[rows, :])
            b_last = b[CHUNK - 1:CHUNK, :]
            q = q_ref[rows, :]
            k = k_ref[rows, :]
            q_dec = q * jnp.exp(b) * scale
            k_inv = k * jnp.exp(-b)
            k_end = k * jnp.exp(b_last - b)
            decay = jnp.exp(b_last)
            vs = [v_ref[rows, vl] for vl in vls]
            scores = [_dot_nt(q_dec[:, kl], jnp.concatenate([k_inv[:, kl], zeros_c], axis=0))
                      for kl in kls]
            kvs = [jnp.dot(k_end[:, kls[hd]].T.astype(BF16), vs[hd], preferred_element_type=F32)
                   for hd in heads]
            decs = [jnp.broadcast_to(decay[:, kl], (GLA_DK, GLA_DK)).T for kl in kls]
            prep.append((rows, q_dec, vs, scores, kvs, decs))
        sts = [state_ref[hd] for hd in heads]
        for rows, q_dec, vs, scores, kvs, decs in prep:
            for hd in heads:
                a = jnp.where(causal, scores[hd], 0.0)
                qa = jnp.concatenate([q_dec[:, kls[hd]], a], axis=-1).astype(BF16)
                rhs = jnp.concatenate([sts[hd].astype(BF16), vs[hd], zeros_v], axis=0)
                att_ref[rows, vls[hd]] = jnp.dot(qa, rhs, preferred_element_type=F32)
            sts = [sts[hd] * jnp.concatenate([decs[hd], decs[hd]], axis=-1) + kvs[hd] for hd in heads]
        for hd in heads:
            state_ref[hd] = sts[hd]

    for grp in range(n_chunks // group):
        group_body(grp)

    o = att_ref[...]
    parts = []
    for hd in range(GLA_HEADS):
        oh = o[:, hd * GLA_DV:(hd + 1) * GLA_DV]
        parts.append(oh * lax.rsqrt(jnp.mean(oh * oh, axis=-1, keepdims=True) + EPS))
    on = jnp.concatenate(parts, axis=-1) * gw_ref[...]
    r = jnp.dot(h_ref[...], wr_ref[...], preferred_element_type=F32)
    y = on * (r * jax.nn.sigmoid(r))
    o_ref[...] = x_ref[...] + jnp.dot(y.astype(BF16), wo_ref[...], preferred_element_type=F32)


def _gla_layer(x, nw, w_in, w_gk, b_gk, gnorm_w, w_out, *, ts):
    bsz, s, d = x.shape
    key = GLA_HEADS * GLA_DK
    val = GLA_HEADS * GLA_DV
    rank = w_gk.shape[0]
    assert s % ts == 0 and ts % CHUNK == 0
    w = w_in.astype(BF16)
    wq, wk = w[:, :key], w[:, key:2 * key]
    wv, wr = w[:, 2 * key:2 * key + val], w[:, 2 * key + val:2 * key + 2 * val]
    wz = jnp.pad(w[:, 2 * key + 2 * val:], ((0, 0), (0, LANES - rank)))
    wgk = jnp.pad(w_gk.astype(BF16), ((0, LANES - rank), (0, 0)))
    n_chunks = ts // CHUNK
    group = 4
    assert n_chunks % group == 0
    kern = functools.partial(_gla_kernel, n_chunks=n_chunks, group=group)
    xspec = pl.BlockSpec((None, ts, d), lambda b, j: (b, j, 0))
    return pl.pallas_call(
        kern,
        grid=(bsz, s // ts),
        in_specs=[
            xspec,
            _resident((1, d)),
            _resident((d, key)), _resident((d, key)), _resident((d, val)), _resident((d, val)),
            _resident((d, LANES)), _resident((LANES, key)), _resident((1, key)),
            _resident((1, val)), _resident((val, d)),
        ],
        out_specs=xspec,
        out_shape=jax.ShapeDtypeStruct(x.shape, F32),
        scratch_shapes=[
            pltpu.VMEM((ts, d), BF16),
            pltpu.VMEM((ts, key), F32),
            pltpu.VMEM((ts, key), F32),
            pltpu.VMEM((ts, key), F32),
            pltpu.VMEM((ts, val), BF16),
            pltpu.VMEM((ts, val), F32),
            pltpu.VMEM((GLA_HEADS, GLA_DK, GLA_DV), F32),
        ],
        compiler_params=pltpu.CompilerParams(
            dimension_semantics=("arbitrary", "arbitrary"), vmem_limit_bytes=VMEM_LIMIT_BYTES),
        name="gla_layer",
    )(x, nw.reshape(1, d), wq, wk, wv, wr, wz, wgk, b_gk.reshape(1, key),
      jnp.tile(gnorm_w, GLA_HEADS).reshape(1, val), w_out.astype(BF16))


def _gdn_kernel(x_ref, nw_ref, wqkv_ref, wr_ref, wab_ref, cw_ref, alog_ref, dtb_ref,
                gw_ref, wo_ref, o_ref,
                h_ref, q_ref, k_ref, v_ref, g_ref, beta_ref, u_ref, wq_ref, kendt_ref, qk_ref,
                elast_ref, ext_ref, state_ref, *, n_chunks, group):
    ts = x_ref.shape[0]
    n_slabs = ext_ref.shape[0]

    @pl.when(pl.program_id(1) == 0)
    def _():
        state_ref[...] = jnp.zeros_like(state_ref)
        ext_ref[:, 0:SUBLANES, :] = jnp.zeros((n_slabs, SUBLANES, LANES), F32)

    @pl.when(pl.program_id(1) != 0)
    def _():
        ext_ref[:, 0:SUBLANES, :] = ext_ref[:, ts:ts + SUBLANES, :]

    h = _rmsnorm(x_ref[...], nw_ref[...]).astype(BF16)
    h_ref[...] = h

    for s in range(0, n_slabs, MXU_N // LANES):
        pre = jnp.dot(h, wqkv_ref[:, s * LANES:s * LANES + MXU_N], preferred_element_type=F32)
        for j in range(MXU_N // LANES):
            ext_ref[s + j, SUBLANES:, :] = pre[:, j * LANES:(j + 1) * LANES]

    scale = GDN_DK ** -0.5
    rows_ph = ts // CONV_PHASES
    for s in range(n_slabs):
        cw = cw_ref[:, s * LANES:(s + 1) * LANES]
        first = SUBLANES - (CONV_K - 1)
        taps = {start: ext_ref[s, pl.ds(start, rows_ph, stride=CONV_PHASES), :]
                for start in range(first, SUBLANES + CONV_PHASES)}
        for r in range(CONV_PHASES):
            conv = taps[SUBLANES + r] * cw[CONV_K - 1:CONV_K, :]
            for back in range(1, CONV_K):
                conv = conv + taps[SUBLANES + r - back] * cw[CONV_K - 1 - back:CONV_K - back, :]
            y = conv * jax.nn.sigmoid(conv)
            dst = pl.ds(r, rows_ph, stride=CONV_PHASES)
            if s < GDN_HEADS:
                y = y * lax.rsqrt(jnp.sum(y * y, axis=-1, keepdims=True) + EPS) * scale
                q_ref[s, dst, :] = y
            elif s < 2 * GDN_HEADS:
                y = y * lax.rsqrt(jnp.sum(y * y, axis=-1, keepdims=True) + EPS)
                k_ref[s - GDN_HEADS, dst, :] = y
            else:
                v_ref[s - 2 * GDN_HEADS, dst, :] = y

    a = jnp.dot(h, wab_ref[...], preferred_element_type=F32)
    bl = pltpu.roll(a, LANES - GDN_HEADS, axis=1)
    beta_ref[...] = jax.nn.sigmoid(bl)
    g_ref[...] = -jnp.exp(alog_ref[...]) * _softplus(a + dtb_ref[...])

    row = lax.broadcasted_iota(jnp.int32, (CHUNK, LANES), 0)
    lane = lax.broadcasted_iota(jnp.int32, (CHUNK, LANES), 1)
    causal = row >= lane
    strict = row > lane
    upper = lane >= CHUNK
    eye_upper = jnp.where(lane == row + CHUNK, 1.0, 0.0)
    zeros_c = jnp.zeros((CHUNK, LANES), F32)
    zeros_rhs = jnp.zeros((CHUNK, GDN_DV + GDN_DK), F32)
    heads = range(GDN_HEADS)
    hls = [slice(hd * GDN_DK, (hd + 1) * GDN_DK) for hd in heads]

    def solve_stages(chunk_ids):
        chains = [(j, hd) for j in range(len(chunk_ids)) for hd in heads]
        t = {}

        def grams():
            rows = [_rows(c, CHUNK) for c in chunk_ids]
            rows2 = [_rows(c, 2 * CHUNK) for c in chunk_ids]
            dcum, dcum_t, beta, e_d, e_end = [], [], [], [], []
            for j, c in enumerate(chunk_ids):
                dcum.append(_cumsum_rows(g_ref[rows[j], :]))
                dcum_t.append(jnp.concatenate([dcum[j], zeros_c], axis=0).T)
                beta.append(beta_ref[rows[j], :])
                e_d.append(jnp.exp(dcum[j]))
                d_last = dcum[j][CHUNK - 1:CHUNK, :]
                e_end.append(jnp.exp(d_last - dcum[j]))
                elast_ref[_rows(c, SUBLANES), :] = jnp.broadcast_to(jnp.exp(d_last), (SUBLANES, LANES))
            ks = [k_ref[hd, rows[j], :] for j, hd in chains]
            qs = [q_ref[hd, rows[j], :] for j, hd in chains]
            kbs = [ks[i] * beta[j][:, hd:hd + 1] for i, (j, hd) in enumerate(chains)]
            t["grams"] = [_dot_nt(jnp.concatenate([kbs[i], qs[i]], axis=0),
                                  jnp.concatenate([ks[i], zeros_c], axis=0)) for i in range(len(chains))]
            t.update(rows=rows, rows2=rows2, dcum=dcum, dcum_t=dcum_t, beta=beta, e_d=e_d, e_end=e_end,
                     ks=ks, qs=qs, kbs=kbs)

        def double():
            prods = [_dot(w, jnp.concatenate([w, zeros_c], axis=0)) for w in t["ws"]]
            t["ws"] = [p + jnp.where(upper, w, 0.0) for p, w in zip(prods, t["ws"])]

        def mask_and_double():
            ws = []
            for i, (j, hd) in enumerate(chains):
                decay = jnp.exp(jnp.minimum(
                    t["dcum"][j][:, hd:hd + 1] - t["dcum_t"][j][hd:hd + 1, :], 0.0))
                gram = t["grams"][i]
                qk_ref[t["rows"][j], hls[hd]] = jnp.where(causal, gram[CHUNK:] * decay, 0.0).astype(BF16)
                ws.append(jnp.where(strict, -(gram[:CHUNK] * decay), 0.0) + eye_upper)
            t["ws"] = ws
            double()

        def apply():
            for i, (j, hd) in enumerate(chains):
                rows, rows2, e_d = t["rows"][j], t["rows2"][j], t["e_d"][j]
                rhs = jnp.concatenate([v_ref[hd, rows, :] * t["beta"][j][:, hd:hd + 1],
                                       t["kbs"][i] * e_d[:, hd:hd + 1]], axis=-1)
                sol = _dot(t["ws"][i], jnp.concatenate([zeros_rhs, rhs], axis=0))
                u_ref[rows, hls[hd]] = sol[:, :GDN_DV]
                wq_ref[rows2, hls[hd]] = jnp.concatenate(
                    [sol[:, GDN_DV:], t["qs"][i] * e_d[:, hd:hd + 1]], axis=0).astype(BF16)
                kend_t = (t["ks"][i] * t["e_end"][j][:, hd:hd + 1]).T
                kendt_ref[rows2, hd * LANES:hd * LANES + CHUNK] = kend_t.astype(BF16)

        n_double = CHUNK.bit_length() - 1
        return [grams, mask_and_double] + [double] * (n_double - 1) + [apply]

    def scan_stages(chunk_ids):
        t = {}

        def project(c, first):
            if first:
                t["sts"] = [state_ref[hd] for hd in heads]
            rows2 = _rows(c, 2 * CHUNK)
            t["wqs"] = [_dot(wq_ref[rows2, hls[hd]], t["sts"][hd]) for hd in heads]

        def update(c, last):
            rows, rows2 = _rows(c, CHUNK), _rows(c, 2 * CHUNK)
            e_last = elast_ref[_rows(c, SUBLANES), :][0:1, :]
            v_news = [u_ref[rows, hls[hd]] - t["wqs"][hd][:CHUNK] for hd in heads]
            for hd in heads:
                qk = qk_ref[rows, hd * LANES:hd * LANES + CHUNK]
                u_ref[rows, hls[hd]] = t["wqs"][hd][CHUNK:] + _dot(qk, v_news[hd])
            sts = []
            for hd in heads:
                kend_t = kendt_ref[rows2, hd * LANES:hd * LANES + CHUNK]
                sts.append(t["sts"][hd] * e_last[:, hd:hd + 1] + _dot(kend_t, v_news[hd]))
            t["sts"] = sts
            if last:
                for hd in heads:
                    state_ref[hd] = sts[hd]

        stages = []
        for idx, c in enumerate(chunk_ids):
            stages.append(functools.partial(project, c, idx == 0))
            stages.append(functools.partial(update, c, idx == len(chunk_ids) - 1))
        return stages

    n_groups = n_chunks // group
    group_ids = lambda g: [g * group + j for j in range(group)]
    _run(solve_stages(group_ids(0)))

    for g in range(n_groups - 1):
        _run(_interleave(solve_stages(group_ids(g + 1)), scan_stages(group_ids(g))))
    _run(scan_stages(group_ids(n_groups - 1)))
    att_ref = u_ref

    o = att_ref[...]
    parts = []
    for hd in range(GDN_HEADS):
        oh = o[:, hd * GDN_DV:(hd + 1) * GDN_DV]
        parts.append(oh * lax.rsqrt(jnp.mean(oh * oh, axis=-1, keepdims=True) + EPS))
    on = jnp.concatenate(parts, axis=-1) * gw_ref[...]
    r = jnp.dot(h_ref[...], wr_ref[...], preferred_element_type=F32)
    y = on * (r * jax.nn.sigmoid(r))
    o_ref[...] = x_ref[...] + jnp.dot(y.astype(BF16), wo_ref[...], preferred_element_type=F32)


def _gdn_layer(x, nw, w_in, conv_w, a_log, dt_bias, gnorm_w, w_out, *, ts):
    bsz, s, d = x.shape
    key = GDN_HEADS * GDN_DK
    val = GDN_HEADS * GDN_DV
    conv_ch = 2 * key + val
    assert s % ts == 0 and ts % CHUNK == 0
    w = w_in.astype(BF16)
    wqkv, wr = w[:, :conv_ch], w[:, conv_ch:conv_ch + val]
    pad = ((0, 0), (0, LANES - GDN_HEADS))
    wab = jnp.pad(w[:, conv_ch + val:], ((0, 0), (0, LANES - 2 * GDN_HEADS)))
    alog = jnp.pad(a_log.reshape(1, GDN_HEADS), pad)
    dtb = jnp.pad(dt_bias.reshape(1, GDN_HEADS), pad)
    n_chunks = ts // CHUNK
    group = 2
    assert n_chunks % group == 0
    kern = functools.partial(_gdn_kernel, n_chunks=n_chunks, group=group)
    xspec = pl.BlockSpec((None, ts, d), lambda b, j: (b, j, 0))
    return pl.pallas_call(
        kern,
        grid=(bsz, s // ts),
        in_specs=[
            xspec,
            _resident((1, d)),
            _resident((d, conv_ch)), _resident((d, val)), _resident((d, LANES)),
            _resident((CONV_K, conv_ch)), _resident((1, LANES)), _resident((1, LANES)),
            _resident((1, val)), _resident((val, d)),
        ],
        out_specs=xspec,
        out_shape=jax.ShapeDtypeStruct(x.shape, F32),
        scratch_shapes=[
            pltpu.VMEM((ts, d), BF16),
            pltpu.VMEM((GDN_HEADS, ts, GDN_DK), F32),
            pltpu.VMEM((GDN_HEADS, ts, GDN_DK), F32),
            pltpu.VMEM((GDN_HEADS, ts, GDN_DV), F32),
            pltpu.VMEM((ts, LANES), F32),
            pltpu.VMEM((ts, LANES), F32),
            pltpu.VMEM((ts, val), F32),
            pltpu.VMEM((2 * ts, key), BF16),
            pltpu.VMEM((2 * ts, key), BF16),
            pltpu.VMEM((ts, key), BF16),
            pltpu.VMEM((ts // CHUNK * SUBLANES, LANES), F32),
            pltpu.VMEM((conv_ch // LANES, ts + SUBLANES, LANES), F32),
            pltpu.VMEM((GDN_HEADS, GDN_DK, GDN_DV), F32),
        ],
        compiler_params=pltpu.CompilerParams(
            dimension_semantics=("arbitrary", "arbitrary"), vmem_limit_bytes=VMEM_LIMIT_BYTES),
        name="gdn_layer",
    )(x, nw.reshape(1, d), wqkv, wr, wab, conv_w, alog, dtb,
      jnp.tile(gnorm_w, GDN_HEADS).reshape(1, val), w_out.astype(BF16))


def _trunk(x, norm_w, ffn_w_in, ffn_w_out, gla_w_in, gla_w_gk, gla_b_gk, gla_norm_w, gla_w_out,
           gdn_w_in, gdn_conv_w, gdn_a_log, gdn_dt_bias, gdn_norm_w, gdn_w_out, final_norm_w):
    bsz, s, d = x.shape
    depth = norm_w.shape[0]
    ffn_w_in = ffn_w_in.astype(BF16)
    ffn_w_out = ffn_w_out.astype(BF16)
    for i in range(depth):
        x = _ffn(x.reshape(bsz * s, d), norm_w[i, 0], ffn_w_in, ffn_w_out, i, 0,
                 final_norm_w, tm=FFN_ROWS, final_norm=False).reshape(bsz, s, d)
        j = i // 2
        if i % 2 == 0:
            x = _gla_layer(x, norm_w[i, 1], gla_w_in[j], gla_w_gk[j], gla_b_gk[j], gla_norm_w[j],
                           gla_w_out[j], ts=GLA_ROWS)
        else:
            x = _gdn_layer(x, norm_w[i, 1], gdn_w_in[j], gdn_conv_w[j], gdn_a_log[j],
                           gdn_dt_bias[j], gdn_norm_w[j], gdn_w_out[j], ts=GDN_ROWS)
        x = _ffn(x.reshape(bsz * s, d), norm_w[i, 2], ffn_w_in, ffn_w_out, i, 1,
                 final_norm_w, tm=FFN_ROWS, final_norm=(i == depth - 1)).reshape(bsz, s, d)
    return x


def kernel(x, norm_w, ffn_w_in, ffn_w_out, gla_w_in, gla_w_gk, gla_b_gk, gla_norm_w, gla_w_out,
           gdn_w_in, gdn_conv_w, gdn_a_log, gdn_dt_bias, gdn_norm_w, gdn_w_out, final_norm_w):
    return _trunk(x, norm_w, ffn_w_in, ffn_w_out, gla_w_in, gla_w_gk, gla_b_gk, gla_norm_w,
                  gla_w_out, gdn_w_in, gdn_conv_w, gdn_a_log, gdn_dt_bias, gdn_norm_w, gdn_w_out,
                  final_norm_w)
```
